```python
import math, functools
import jax, jax.numpy as jnp
from jax import lax
import numpy as np

D_MODEL = 4096
BATCH = 4
SEQ = 2048
DEPTH = 1
DEC_BATCH = 32
DEC_SEQ = 4
PAST_LEN = 8192
PAGE_SIZE = 128

LIN_WIDTH = D_MODEL // 2
NSA_WIDTH = D_MODEL - LIN_WIDTH
D_MIX = LIN_WIDTH + NSA_WIDTH
LIN_DK = 128
LIN_DV = 128
LIN_HEADS = LIN_WIDTH // LIN_DV
LIN_QK = LIN_HEADS * LIN_DK
LIN_CHUNK = 16
NSA_DK = 128
NSA_HEADS = NSA_WIDTH // NSA_DK
NSA_KV_HEADS = 4
NSA_REP = NSA_HEADS // NSA_KV_HEADS
KV_WIDTH = NSA_KV_HEADS * NSA_DK
CMP_STRIDE = 16
CMP_LEN = 2 * CMP_STRIDE
CMP_HIDDEN = 2 * NSA_DK
SEL_BLOCK = 64
N_SEL = 16
WINDOW = 512
SEL_QCHUNK = 16
WIN_QBLOCK = 128
N_GROUPS = 4
EXP_PER_GROUP = 4
N_EXPERTS = N_GROUPS * EXP_PER_GROUP
EXP_TOPK = 2
D_EXPERT = 1024
PLE_DIM = 256
N_IN = 2 * LIN_QK + 2 * LIN_WIDTH + NSA_WIDTH + 6 * KV_WIDTH + 3 * NSA_HEADS
EPS = 1e-6
NEG = -1e30
FORCE = 1e4

kernel_name = 'hgrn2_nsa_hmoe_ple_step'


def _rms(x, g):
    xf = x.astype(jnp.float32)
    y = xf * lax.rsqrt(jnp.mean(xf * xf, axis=-1, keepdims=True) + EPS)
    return (y * g.astype(jnp.float32)).astype(x.dtype)


def _knorm(kv, g):
    return jnp.stack([_rms(kv[..., 0, :], g), kv[..., 1, :]], axis=-2)


def _alibi_slopes():
    h = jnp.arange(1, NSA_HEADS + 1, dtype=jnp.float32)
    return jnp.exp2(-8.0 * h / NSA_HEADS).reshape(NSA_KV_HEADS, NSA_REP)


def _in_proj(a, w_in, g_q, g_ks, g_kw):
    b, t, _ = a.shape
    z = a @ w_in
    sizes = [LIN_QK, LIN_QK, LIN_WIDTH, LIN_WIDTH, NSA_WIDTH, 2 * KV_WIDTH, 2 * KV_WIDTH, 2 * KV_WIDTH]
    lq, lf, li, lg, nq, kvc, kvs, kvw, gt = jnp.split(z, np.cumsum(sizes).tolist(), axis=-1)
    lin = lambda u, d: u.reshape(b, t, LIN_HEADS, d)
    kv = lambda u: u.reshape(b, t, NSA_KV_HEADS, 2, NSA_DK)
    nq = _rms(nq.reshape(b, t, NSA_KV_HEADS, NSA_REP, NSA_DK), g_q)
    gates = jax.nn.sigmoid(gt.astype(jnp.float32)).reshape(b, t, 3, NSA_KV_HEADS, NSA_REP, 1)
    return (lin(lq, LIN_DK), lin(lf, LIN_DK), lin(li, LIN_DV), lin(lg, LIN_DV), nq,
            kv(kvc), _knorm(kv(kvs), g_ks), _knorm(kv(kvw), g_kw), gates)


def _hgrn2(q, f_logit, inp, lb, s0):
    b, t = q.shape[:2]
    c = math.gcd(t, LIN_CHUNK)
    n = t // c
    z = f_logit.astype(jnp.float32)
    lbf = lb.astype(jnp.float32)
    logf = jnp.log(lbf + (1.0 - lbf) * jax.nn.sigmoid(z))
    k = (1.0 - lbf) * jax.nn.sigmoid(-z)
    chunks = lambda u: u.astype(jnp.float32).reshape(b, n, c, LIN_HEADS, -1).transpose(1, 0, 3, 2, 4)
    causal = jnp.tril(jnp.ones((c, c), dtype=bool))

    def step(S, xs):
        qc, kc, vc, lc = xs
        cb = jnp.cumsum(lc, axis=2)
        decay = jnp.exp(jnp.where(causal[:, :, None], cb[:, :, :, None, :] - cb[:, :, None, :, :], -jnp.inf))
        attn = jnp.einsum('bhtd,bhsd,bhtsd->bhts', qc, kc, decay)
        o = jnp.einsum('bhtd,bhde->bhte', qc * jnp.exp(cb), S) + jnp.einsum('bhts,bhse->bhte', attn, vc)
        last = cb[:, :, -1:, :]
        S = jnp.exp(last[:, :, 0, :, None]) * S + jnp.einsum('bhsd,bhse->bhde', kc * jnp.exp(last - cb), vc)
        return S, o

    S, o = lax.scan(step, s0.astype(jnp.float32), (chunks(q), chunks(k), chunks(inp), chunks(logf)))
    return o.transpose(1, 0, 3, 2, 4).reshape(b, t, LIN_HEADS, LIN_DV), S


def _compress(kv, pos_logits, w1, w2, g_kc):
    b, L = kv.shape[:2]
    r = kv.reshape(b, L // CMP_STRIDE, CMP_STRIDE, NSA_KV_HEADS, 2, NSA_DK)
    a = jax.nn.softmax(pos_logits.astype(jnp.float32), axis=-1).astype(kv.dtype)
    pooled = (jnp.einsum('bclgtd,gtl->bcgtd', r[:, :-1], a[..., :CMP_STRIDE])
              + jnp.einsum('bclgtd,gtl->bcgtd', r[:, 1:], a[..., CMP_STRIDE:]))
    hid = jax.nn.silu(jnp.einsum('bcgtd,tdh->bcgth', pooled, w1))
    out = pooled + jnp.einsum('bcgth,thd->bcgtd', hid, w2)
    return _rms(out[..., 0, :], g_kc), out[..., 1, :]


def _fetch_rows(kv_rows, spos):
    bi = jnp.arange(kv_rows.shape[0])[:, None, None, None, None]
    gi = jnp.arange(NSA_KV_HEADS)[None, None, :, None, None]
    return kv_rows[bi, spos, gi]


def _fetch_paged(pool, page_table, kv_new, spos):
    past_len = page_table.shape[1] * PAGE_SIZE
    bi = jnp.arange(page_table.shape[0])[:, None, None, None, None]
    gi = jnp.arange(NSA_KV_HEADS)[None, None, :, None, None]
    pp = jnp.minimum(spos, past_len - 1)
    past = pool[page_table[bi, pp // PAGE_SIZE], pp % PAGE_SIZE, gi]
    new = kv_new[bi, jnp.clip(spos - past_len, 0, kv_new.shape[1] - 1), gi].astype(past.dtype)
    return jnp.where((spos < past_len)[..., None, None], past, new)


def _sparse_branches(q, q_pos, kc, vc, n_blocks, fetch, slopes):
    b, t = q.shape[:2]
    nc = kc.shape[1]
    qc_len = math.gcd(t, SEL_QCHUNK)
    nq = t // qc_len
    k_sel = min(N_SEL, n_blocks)
    scale = NSA_DK ** -0.5
    cstart = jnp.arange(nc) * CMP_STRIDE
    cend = cstart + CMP_LEN - 1
    jb = jnp.arange(n_blocks)
    bstart = jb * SEL_BLOCK
    overlap = ((cstart[:, None] < bstart[None, :] + SEL_BLOCK)
               & (cstart[:, None] + CMP_LEN > bstart[None, :])).astype(jnp.float32)

    def chunk(args):
        qb, qp = args
        s = jnp.einsum('bqgrd,bcgd->bqgrc', qb, kc).astype(jnp.float32) * scale
        dist = (qp[:, None] - cend[None, :]).astype(jnp.float32)
        valid = (qp[:, None] >= cend[None, :])[None, :, None, None, :]
        s = s - slopes[None, None, :, :, None] * dist[None, :, None, None, :]
        p = jax.nn.softmax(jnp.where(valid, s, NEG), axis=-1) * valid
        o_cmp = jnp.einsum('bqgrc,bcgd->bqgrd', p.astype(vc.dtype), vc)
        imp = jnp.einsum('bqgrc,cn->bqgn', p, overlap)
        qblk = qp // SEL_BLOCK
        future = (bstart[None, :] > qp[:, None])[None, :, None, :]
        forced = ((jb[None, :] == 0) | (jb[None, :] == qblk[:, None]) | (jb[None, :] == qblk[:, None] - 1))[None, :, None, :]
        score = jnp.where(forced, FORCE, jnp.where(future, -1.0, imp))
        _, idx = lax.top_k(score, k_sel)
        spos = idx[..., None] * SEL_BLOCK + jnp.arange(SEL_BLOCK)
        kv = fetch(spos)
        ks, vs = kv[..., 0, :], kv[..., 1, :]
        s2 = jnp.einsum('bqgrd,bqgnld->bqgrnl', qb, ks).astype(jnp.float32) * scale
        d2 = qp[None, :, None, None, None] - spos
        s2 = s2 - slopes[None, None, :, :, None, None] * d2[:, :, :, None].astype(jnp.float32)
        s2 = jnp.where((d2 >= 0)[:, :, :, None], s2, NEG)
        p2 = jax.nn.softmax(s2.reshape(s2.shape[:4] + (-1,)), axis=-1).reshape(s2.shape)
        o_sel = jnp.einsum('bqgrnl,bqgnld->bqgrd', p2.astype(vs.dtype), vs)
        return o_cmp, o_sel

    qs = q.reshape(b, nq, qc_len, NSA_KV_HEADS, NSA_REP, NSA_DK).swapaxes(0, 1)
    o_cmp, o_sel = lax.map(chunk, (qs, q_pos.reshape(nq, qc_len)))
    unchunk = lambda o: o.swapaxes(0, 1).reshape(b, t, NSA_KV_HEADS, NSA_REP, NSA_DK)
    return unchunk(o_cmp), unchunk(o_sel)


def _window_attn(q, kv, q_pos, k_pos, slopes):
    k, v = kv[..., 0, :], kv[..., 1, :]
    s = jnp.einsum('...qgrd,...kgd->...grqk', q, k).astype(jnp.float32) * NSA_DK ** -0.5
    dist = q_pos[..., :, None] - k_pos[..., None, :]
    valid = (dist >= 0) & (dist < WINDOW) & (k_pos[..., None, :] >= 0)
    s = s - slopes[:, :, None, None] * dist[..., None, None, :, :].astype(jnp.float32)
    p = jax.nn.softmax(jnp.where(valid[..., None, None, :, :], s, NEG), axis=-1)
    return jnp.einsum('...grqk,...kgd->...qgrd', p.astype(v.dtype), v)


def _window_prompt(q, kv, slopes):
    b, t = q.shape[:2]
    nb = t // WIN_QBLOCK
    nsh = WINDOW // WIN_QBLOCK
    kvp = jnp.pad(kv, ((0, 0), (WINDOW, 0), (0, 0), (0, 0), (0, 0)))
    kvp = kvp.reshape(b, nb + nsh, WIN_QBLOCK, NSA_KV_HEADS, 2, NSA_DK)
    kvb = jnp.concatenate([kvp[:, j:j + nb] for j in range(nsh + 1)], axis=2)
    qb = q.reshape(b, nb, WIN_QBLOCK, NSA_KV_HEADS, NSA_REP, NSA_DK)
    qpos = jnp.arange(t).reshape(nb, WIN_QBLOCK)
    kpos = jnp.arange(nb)[:, None] * WIN_QBLOCK - WINDOW + jnp.arange((nsh + 1) * WIN_QBLOCK)[None, :]
    return _window_attn(qb, kvb, qpos, kpos, slopes).reshape(b, t, NSA_KV_HEADS, NSA_REP, NSA_DK)


def _mixer_out(o_lin, lg, o_cmp, o_sel, o_win, gates, g_lin_out, g_nsa_out, w_out):
    b, t = lg.shape[:2]
    y_lin = (_rms(o_lin, g_lin_out) * jax.nn.silu(lg.astype(jnp.float32))).reshape(b, t, LIN_WIDTH)
    o_nsa = gates[:, :, 0] * o_cmp + gates[:, :, 1] * o_sel + gates[:, :, 2] * o_win
    y_nsa = _rms(o_nsa.reshape(b, t, NSA_WIDTH), g_nsa_out)
    return jnp.concatenate([y_lin, y_nsa], axis=-1).astype(w_out.dtype) @ w_out


def _moe(m, w_gr, b_gr, w_er, b_er, w_g, w_u, w_d):
    shp = m.shape
    x = m.reshape(-1, D_MODEL)
    glog = (x @ w_gr).astype(jnp.float32) + b_gr.astype(jnp.float32)
    gsel = jnp.argmax(glog, axis=-1)
    gw = jnp.take_along_axis(jax.nn.softmax(glog, axis=-1), gsel[:, None], axis=1)[:, 0]
    elog = ((x @ w_er).astype(jnp.float32) + b_er.astype(jnp.float32)).reshape(-1, N_GROUPS, EXP_PER_GROUP)
    elog = jnp.take_along_axis(elog, gsel[:, None, None], axis=1)[:, 0]
    tv, ti = lax.top_k(elog, EXP_TOPK)
    ew = jax.nn.softmax(tv, axis=-1) * gw[:, None]
    eidx = gsel[:, None] * EXP_PER_GROUP + ti
    comb = jnp.sum(jax.nn.one_hot(eidx, N_EXPERTS, dtype=jnp.float32) * ew[..., None], axis=1)
    hid = jax.nn.silu(jnp.einsum('nd,edf->nef', x, w_g)) * jnp.einsum('nd,edf->nef', x, w_u)
    y = jnp.einsum('nef,efd->nd', hid * comb[..., None].astype(hid.dtype), w_d)
    return y.reshape(shp)


def _ffn_ple(h, p_l, g_ffn, w_gr, b_gr, w_er, b_er, w_g, w_u, w_d, g_pin, w_pg, w_pp, g_pout):
    h = h + _moe(_rms(h, g_ffn), w_gr, b_gr, w_er, b_er, w_g, w_u, w_d)
    gate = jax.nn.sigmoid(_rms(h, g_pin) @ w_pg)
    return h + gate * _rms(p_l @ w_pp, g_pout)


def setup_inputs(seed: int = 0) -> dict:
    key = jax.random.key(seed)
    keys = iter(jax.random.split(key, 48))

    def nrm(shape, scale):
        return jax.random.normal(next(keys), shape, jnp.float32) * scale

    def gain(shape):
        return 1.0 + nrm(shape, 0.02)

    n_pages = PAST_LEN // PAGE_SIZE
    n_used = DEC_BATCH * n_pages
    n_phys = n_used + n_used // 4
    w_keep = min(WINDOW, PAST_LEN)
    page_table = jax.random.permutation(next(keys), n_phys)[:n_used].reshape(DEC_BATCH, n_pages).astype(jnp.int32)
    kvd = (NSA_KV_HEADS, 2, NSA_DK)
    return {
        'x_prompt': nrm((BATCH, SEQ, D_MODEL), 1.0),
        'x_sample': nrm((DEC_BATCH, DEC_SEQ, D_MODEL), 1.0),
        'cache_cmp': nrm((DEPTH, n_phys, PAGE_SIZE) + kvd, 1.0),
        'cache_sel': nrm((DEPTH, n_phys, PAGE_SIZE) + kvd, 1.0),
        'cache_win': nrm((DEPTH, DEC_BATCH, w_keep) + kvd, 1.0),
        'state_hgrn': nrm((DEPTH, DEC_BATCH, LIN_HEADS, LIN_DK, LIN_DV), 0.5),
        'page_table': page_table,
        'p_prompt': nrm((DEPTH, BATCH, SEQ, PLE_DIM), 1.0),
        'p_sample': nrm((DEPTH, DEC_BATCH, DEC_SEQ, PLE_DIM), 1.0),
        'g_attn': gain((DEPTH, D_MODEL)),
        'w_in': nrm((DEPTH, D_MODEL, N_IN), D_MODEL ** -0.5),
        'lb_logits': nrm((DEPTH + 1, LIN_HEADS, LIN_DK), 0.5),
        'g_lin_out': gain((DEPTH, LIN_DV)),
        'g_q': gain((DEPTH, NSA_DK)),
        'g_kc': gain((DEPTH, NSA_DK)),
        'g_ks': gain((DEPTH, NSA_DK)),
        'g_kw': gain((DEPTH, NSA_DK)),
        'cmp_pos_logits': nrm((DEPTH, NSA_KV_HEADS, 2, CMP_LEN), 0.5),
        'cmp_w1': nrm((DEPTH, 2, NSA_DK, CMP_HIDDEN), NSA_DK ** -0.5),
        'cmp_w2': nrm((DEPTH, 2, CMP_HIDDEN, NSA_DK), 0.5 * CMP_HIDDEN ** -0.5),
        'g_nsa_out': gain((DEPTH, NSA_WIDTH)),
        'w_out': nrm((DEPTH, D_MIX, D_MODEL), D_MIX ** -0.5),
        'g_ffn': gain((DEPTH, D_MODEL)),
        'w_group_router': nrm((DEPTH, D_MODEL, N_GROUPS), D_MODEL ** -0.5),
        'b_group_router': nrm((DEPTH, N_GROUPS), 0.01),
        'w_expert_router': nrm((DEPTH, D_MODEL, N_EXPERTS), D_MODEL ** -0.5),
        'b_expert_router': nrm((DEPTH, N_EXPERTS), 0.01),
        'w_exp_gate': nrm((DEPTH, N_EXPERTS, D_MODEL, D_EXPERT), D_MODEL ** -0.5),
        'w_exp_up': nrm((DEPTH, N_EXPERTS, D_MODEL, D_EXPERT), D_MODEL ** -0.5),
        'w_exp_down': nrm((DEPTH, N_EXPERTS, D_EXPERT, D_MODEL), D_EXPERT ** -0.5),
        'g_ple_in': gain((DEPTH, D_MODEL)),
        'w_ple_gate': nrm((DEPTH, D_MODEL, D_MODEL), D_MODEL ** -0.5),
        'w_ple_proj': nrm((DEPTH, PLE_DIM, D_MODEL), PLE_DIM ** -0.5),
        'g_ple_out': gain((DEPTH, D_MODEL)),
    }


def reference(x_prompt, x_sample, cache_cmp, cache_sel, cache_win, state_hgrn, page_table, p_prompt, p_sample,
              g_attn, w_in, lb_logits, g_lin_out, g_q, g_kc, g_ks, g_kw, cmp_pos_logits, cmp_w1, cmp_w2,
              g_nsa_out, w_out, g_ffn, w_group_router, b_group_router, w_expert_router, b_expert_router,
              w_exp_gate, w_exp_up, w_exp_down, g_ple_in, w_ple_gate, w_ple_proj, g_ple_out):
    slopes = _alibi_slopes()
    lb_all = jnp.cumsum(jax.nn.softmax(lb_logits.astype(jnp.float32), axis=0), axis=0)
    past_len = page_table.shape[1] * PAGE_SIZE
    w_keep = cache_win.shape[2]
    bp, tp = x_prompt.shape[:2]
    bs, ts = x_sample.shape[:2]
    hp, hs = x_prompt, x_sample
    cmp_p, cmp_s, sel_p, sel_s, win_p, win_s, st_p, st_s = [], [], [], [], [], [], [], []
    for l in range(DEPTH):
        ffn_w = (g_ffn[l], w_group_router[l], b_group_router[l], w_expert_router[l], b_expert_router[l],
                 w_exp_gate[l], w_exp_up[l], w_exp_down[l], g_ple_in[l], w_ple_gate[l], w_ple_proj[l], g_ple_out[l])
        lq, lf, li, lg, nq, kvc, kvs, kvw, gates = _in_proj(_rms(hp, g_attn[l]), w_in[l], g_q[l], g_ks[l], g_kw[l])
        s0 = jnp.zeros((bp, LIN_HEADS, LIN_DK, LIN_DV), jnp.float32)
        o_lin, s_new = _hgrn2(lq, lf, li, lb_all[l], s0)
        kc, vc = _compress(kvc, cmp_pos_logits[l], cmp_w1[l], cmp_w2[l], g_kc[l])
        o_cmp, o_sel = _sparse_branches(nq, jnp.arange(tp), kc, vc, tp // SEL_BLOCK,
                                        functools.partial(_fetch_rows, kvs), slopes)
        o_win = _window_prompt(nq, kvw, slopes)
        hp = hp + _mixer_out(o_lin, lg, o_cmp, o_sel, o_win, gates, g_lin_out[l], g_nsa_out[l], w_out[l])
        hp = _ffn_ple(hp, p_prompt[l], *ffn_w)
        cmp_p.append(kvc)
        sel_p.append(kvs)
        win_p.append(kvw[:, tp - min(WINDOW, tp):])
        st_p.append(s_new.astype(x_prompt.dtype))
        lq, lf, li, lg, nq, kvc, kvs, kvw, gates = _in_proj(_rms(hs, g_attn[l]), w_in[l], g_q[l], g_ks[l], g_kw[l])
        o_lin, s_new = _hgrn2(lq, lf, li, lb_all[l], state_hgrn[l])
        past_cmp = cache_cmp[l][page_table].reshape(bs, past_len, NSA_KV_HEADS, 2, NSA_DK)
        total = past_len + ts
        padded = -(-total // CMP_STRIDE) * CMP_STRIDE
        full = jnp.concatenate([past_cmp, kvc.astype(past_cmp.dtype)], axis=1)
        full = jnp.pad(full, ((0, 0), (0, padded - total), (0, 0), (0, 0), (0, 0)))
        kc, vc = _compress(full, cmp_pos_logits[l], cmp_w1[l], cmp_w2[l], g_kc[l])
        q_pos = past_len + jnp.arange(ts)
        fetch = functools.partial(_fetch_paged, cache_sel[l], page_table, kvs)
        o_cmp, o_sel = _sparse_branches(nq, q_pos, kc, vc, -(-total // SEL_BLOCK), fetch, slopes)
        kw_all = jnp.concatenate([cache_win[l], kvw.astype(cache_win.dtype)], axis=1)
        k_pos = jnp.arange(past_len - w_keep, past_len + ts)
        o_win = _window_attn(nq, kw_all, q_pos, k_pos, slopes)
        hs = hs + _mixer_out(o_lin, lg, o_cmp, o_sel, o_win, gates, g_lin_out[l], g_nsa_out[l], w_out[l])
        hs = _ffn_ple(hs, p_sample[l], *ffn_w)
        cmp_s.append(kvc)
        sel_s.append(kvs)
        win_s.append(kw_all[:, kw_all.shape[1] - w_keep:])
        st_s.append(s_new.astype(state_hgrn.dtype))
    cmp_rows_prompt = jnp.stack(cmp_p, axis=0)
    cmp_rows_sample = jnp.stack(cmp_s, axis=0)
    sel_rows_prompt = jnp.stack(sel_p, axis=0)
    sel_rows_sample = jnp.stack(sel_s, axis=0)
    win_prompt = jnp.stack(win_p, axis=0)
    win_sample = jnp.stack(win_s, axis=0)
    hgrn_prompt = jnp.stack(st_p, axis=0)
    hgrn_sample = jnp.stack(st_s, axis=0)
    return (hp, hs, cmp_rows_prompt, cmp_rows_sample, sel_rows_prompt, sel_rows_sample,
            win_prompt, win_sample, hgrn_prompt, hgrn_sample)
```

```python
import functools
import math

import numpy as np
import jax
import jax.numpy as jnp
from jax import lax
from jax.experimental import pallas as pl
from jax.experimental.pallas import tpu as pltpu

F32 = jnp.float32
BF16 = jnp.bfloat16

D_MODEL = 4096
BATCH = 4
SEQ = 2048
DEC_BATCH = 32
DEC_SEQ = 4
PAST_LEN = 8192
PAGE_SIZE = 128
N_PAGES = PAST_LEN // PAGE_SIZE
LIN_WIDTH = 2048
NSA_WIDTH = 2048
HEAD_DIM = 128
LIN_HEADS = 16
NSA_HEADS = 16
NSA_KV_HEADS = 4
NSA_REP = 4
KV_WIDTH = NSA_KV_HEADS * HEAD_DIM
KV_ROW = 2 * KV_WIDTH
KV_SUB = 2 * NSA_KV_HEADS
CMP_STRIDE = 16
CMP_LEN = 32
CMP_HIDDEN = 256
SEL_BLOCK = 64
N_SEL = 16
WINDOW = 512
N_EXPERTS = 16
N_GROUPS = 4
EXP_PER_GROUP = 4
D_EXPERT = 1024
PLE_DIM = 256
EPS = 1e-6
NEG = -1e30
FORCE = 1e4
SCALE = HEAD_DIM ** -0.5

COL_LQ, COL_LF, COL_LI, COL_LG = 0, 2048, 4096, 6144
COL_NQ = 8192
COL_KVC, COL_KVS, COL_KVW = 10240, 11264, 12288
COL_GATE = 13312
N_MAIN = 13312
N_GATE = 3 * NSA_HEADS

NP = BATCH * SEQ
SROW = 16
NS = DEC_BATCH * SROW
NT = NP + NS

VMEM_LIMIT = 56 * 1024 * 1024

SLOPES = [2.0 ** (-8.0 * (h + 1) / NSA_HEADS) for h in range(NSA_HEADS)]


def _cparams(sem):
    return pltpu.CompilerParams(dimension_semantics=sem, vmem_limit_bytes=VMEM_LIMIT)


def _dot(a, b):
    return jnp.dot(a, b, preferred_element_type=F32)


def _dot_nt(a, b):
    return lax.dot_general(a, b, (((1,), (1,)), ((), ())), preferred_element_type=F32)


def _dot_tn(a, b):
    return lax.dot_general(a, b, (((0,), (0,)), ((), ())), preferred_element_type=F32)


def _split3(x):
    a = x.astype(BF16)
    r = x - a.astype(F32)
    b = r.astype(BF16)
    c = (r - b.astype(F32)).astype(BF16)
    return a, b, c


def _rms_rows(x, g):
    ms = jnp.mean(x * x, axis=-1, keepdims=True)
    return x * lax.rsqrt(ms + EPS) * g


def _silu(x):
    return x * jax.nn.sigmoid(x)


def _rms_cast_kernel(x_ref, g_ref, o_ref):
    o_ref[...] = _rms_rows(x_ref[...], g_ref[...]).astype(o_ref.dtype)


def _rms_cast(x, g, tm):
    n, d = x.shape
    return pl.pallas_call(
        _rms_cast_kernel,
        grid=(n // tm,),
        in_specs=[pl.BlockSpec((tm, d), lambda i: (i, 0)), pl.BlockSpec((1, d), lambda i: (0, 0))],
        out_specs=pl.BlockSpec((tm, d), lambda i: (i, 0)),
        out_shape=jax.ShapeDtypeStruct((n, d), BF16),
        compiler_params=_cparams(("arbitrary",)),
        name="rms_cast",
    )(x, g.reshape(1, d))


def _mm_kernel(flag_ref, x_ref, w_ref, *rest, epilogue, n_extra):
    extras = rest[:n_extra]
    o_ref = rest[n_extra]
    wb_ref = rest[n_extra + 1]
    j = pl.program_id(0)

    @pl.when(pl.program_id(1) == 0)
    def _():
        wb_ref[...] = w_ref[...].astype(BF16)

    acc = _dot(x_ref[...], wb_ref[...])
    if epilogue == "groupnorm":
        gain_ref, sel_ref = extras
        normed = flag_ref[j] > 0

        @pl.when(normed)
        def _():
            for c in range(acc.shape[1] // HEAD_DIM):
                cs = slice(c * HEAD_DIM, (c + 1) * HEAD_DIM)
                blk = acc[:, cs]
                o_ref[:, cs] = jnp.where(sel_ref[:, cs] > 0, _rms_rows(blk, gain_ref[:, cs]), blk)

        @pl.when(jnp.logical_not(normed))
        def _():
            o_ref[...] = acc
    elif epilogue == "sigmoid":
        o_ref[...] = jax.nn.sigmoid(acc)
    elif epilogue == "residual":
        o_ref[...] = extras[0][...] + acc
    elif epilogue == "rowrms":
        o_ref[...] = _rms_rows(acc, extras[0][...])
    elif epilogue == "ple":
        h_ref, pp_ref = extras
        o_ref[...] = h_ref[...] + jax.nn.sigmoid(acc) * pp_ref[...]
    else:
        raise ValueError(epilogue)


def _matmul(x, w, n_out, tm, tn, epilogue, extras=(), extra_kinds=(), flags=None, name="mm"):
    m, k = x.shape
    n_tiles = n_out // tn
    if flags is None:
        flags = jnp.zeros((n_tiles,), jnp.int32)
    specs = [pl.BlockSpec((tm, k), lambda j, i, f: (i, 0)), pl.BlockSpec((k, tn), lambda j, i, f: (0, j))]
    for kind in extra_kinds:
        if kind == "row":
            specs.append(pl.BlockSpec((1, tn), lambda j, i, f: (0, j)))
        else:
            specs.append(pl.BlockSpec((tm, tn), lambda j, i, f: (i, j)))
    return pl.pallas_call(
        functools.partial(_mm_kernel, epilogue=epilogue, n_extra=len(extras)),
        grid_spec=pltpu.PrefetchScalarGridSpec(
            num_scalar_prefetch=1,
            grid=(n_tiles, m // tm),
            in_specs=specs,
            out_specs=pl.BlockSpec((tm, tn), lambda j, i, f: (i, j)),
            scratch_shapes=[pltpu.VMEM((k, tn), BF16)],
        ),
        out_shape=jax.ShapeDtypeStruct((m, n_out), F32),
        compiler_params=_cparams(("arbitrary", "arbitrary")),
        name=name,
    )(flags, x, w, *extras)


def _hgrn_masks(c):
    levels = int(math.log2(c))
    sums = np.zeros((levels + 2, c, c), np.float32)
    pair = np.zeros((levels + 1, c, c), np.float32)
    pair[0] = np.eye(c)
    for lv in range(levels):
        h = 1 << lv
        for t in range(c):
            base = (t // (2 * h)) * 2 * h
            m = base + h - 1
            if t % (2 * h) >= h:
                sums[lv, t, m + 1:t + 1] = 1.0
                pair[lv + 1, t, base:base + h] = 1.0
            else:
                sums[lv, t, t + 1:m + 1] = 1.0
    for t in range(c):
        sums[levels, t, :t + 1] = 1.0
        sums[levels + 1, t, t + 1:] = 1.0
    return sums.reshape((levels + 2) * c, c), pair


def _hgrn_kernel(*refs, c, n_chunks, t_valid, has_s0):
    if has_s0:
        q_ref, f_ref, v_ref, g_ref, lb_ref, gout_ref, sums_ref, pair_ref, s0_ref, y_ref, s_ref, st_scr = refs
        st_scr[...] = s0_ref[...].T
    else:
        q_ref, f_ref, v_ref, g_ref, lb_ref, gout_ref, sums_ref, pair_ref, y_ref, s_ref, st_scr = refs
        st_scr[...] = jnp.zeros_like(st_scr)
    levels = int(math.log2(c))
    lb = lb_ref[...]
    row = lax.broadcasted_iota(jnp.int32, (c, 1), 0)

    def chunk(ci, carry):
        r0 = pl.multiple_of(ci * c, c)
        q = q_ref[pl.ds(r0, c), :]
        z = f_ref[pl.ds(r0, c), :]
        v = v_ref[pl.ds(r0, c), :]
        logf = jnp.log(lb + (1.0 - lb) * jax.nn.sigmoid(z))
        k = (1.0 - lb) * jax.nn.sigmoid(-z)
        if t_valid < c:
            live = row < t_valid
            logf = jnp.where(live, logf, 0.0)
            k = jnp.where(live, k, 0.0)
        parts = jnp.concatenate(_split3(logf), axis=1)
        r = _dot(sums_ref[...], parts)
        ex = jnp.exp(r[:, :HEAD_DIM] + r[:, HEAD_DIM:2 * HEAD_DIM] + r[:, 2 * HEAD_DIM:])
        attn = pair_ref[0] * _dot_nt(q.astype(BF16), k.astype(BF16))
        for lv in range(levels):
            w = ex[lv * c:(lv + 1) * c]
            attn = attn + pair_ref[lv + 1] * _dot_nt((q * w).astype(BF16), (k * w).astype(BF16))
        e_cum = ex[levels * c:(levels + 1) * c]
        e_rev = ex[(levels + 1) * c:(levels + 2) * c]
        st = st_scr[...]
        vb = v.astype(BF16)
        o = _dot_nt((q * e_cum).astype(BF16), st.astype(BF16)) + _dot(attn.astype(BF16), vb)
        st_scr[...] = st * e_cum[c - 1:c, :] + _dot_tn(vb, (k * e_rev).astype(BF16))
        y = _rms_rows(o, gout_ref[...]) * _silu(g_ref[pl.ds(r0, c), :])
        y_ref[pl.ds(r0, c), :] = y
        return carry

    lax.fori_loop(0, n_chunks, chunk, 0)
    s_ref[...] = st_scr[...].T


def _hgrn(z, lb, g_lin_out, s0, *, batch, rows, row0, c, t_valid):
    sums, pair = _hgrn_masks(c)
    blk0 = row0 // rows
    has_s0 = s0 is not None

    def zspec(col):
        cb = col // HEAD_DIM
        return pl.BlockSpec((rows, HEAD_DIM), lambda b, h: (blk0 + b, cb + h))

    const2 = lambda b, h: (0, 0)
    in_specs = [zspec(COL_LQ), zspec(COL_LF), zspec(COL_LI), zspec(COL_LG),
                pl.BlockSpec((None, 1, HEAD_DIM), lambda b, h: (h, 0, 0)),
                pl.BlockSpec((1, HEAD_DIM), const2),
                pl.BlockSpec(sums.shape, const2),
                pl.BlockSpec(pair.shape, lambda b, h: (0, 0, 0))]
    args = [z, z, z, z, lb.reshape(LIN_HEADS, 1, HEAD_DIM), g_lin_out.reshape(1, HEAD_DIM),
            jnp.asarray(sums, BF16), jnp.asarray(pair, F32)]
    if has_s0:
        in_specs.append(pl.BlockSpec((None, None, HEAD_DIM, HEAD_DIM), lambda b, h: (b, h, 0, 0)))
        args.append(s0)
    return pl.pallas_call(
        functools.partial(_hgrn_kernel, c=c, n_chunks=rows // c, t_valid=t_valid, has_s0=has_s0),
        grid=(batch, LIN_HEADS),
        in_specs=in_specs,
        out_specs=[pl.BlockSpec((rows, HEAD_DIM), lambda b, h: (b, h)),
                   pl.BlockSpec((None, None, HEAD_DIM, HEAD_DIM), lambda b, h: (b, h, 0, 0))],
        out_shape=[jax.ShapeDtypeStruct((batch * rows, LIN_WIDTH), F32),
                   jax.ShapeDtypeStruct((batch, LIN_HEADS, HEAD_DIM, HEAD_DIM), F32)],
        scratch_shapes=[pltpu.VMEM((HEAD_DIM, HEAD_DIM), F32)],
        compiler_params=_cparams(("arbitrary", "arbitrary")),
        name="hgrn2",
    )(*args)


def _pool_weights(logit_ref):
    lg = logit_ref[...]
    e = jnp.exp(lg - jnp.max(lg, axis=0, keepdims=True))
    a = e / jnp.sum(e, axis=0, keepdims=True)
    return a[:CMP_STRIDE], a[CMP_STRIDE:]


def _pool(rows, a1, a2):
    n = rows.shape[0] // CMP_STRIDE
    r = rows.reshape(n, CMP_STRIDE, KV_SUB, HEAD_DIM)
    return jnp.sum(r * a1[None], axis=1), jnp.sum(r * a2[None], axis=1)


def _cmp_mlp(pooled, w1_ref, w2_ref, gkc_ref):
    r = pooled.shape[0]
    xs = pooled.reshape(r * KV_SUB, HEAD_DIM)
    xb = xs.astype(BF16)
    outs = []
    for t in range(2):
        hid = _silu(_dot(xb, w1_ref[t].astype(BF16)))
        outs.append(xs + _dot(hid.astype(BF16), w2_ref[t].astype(BF16)))
    is_key = (lax.broadcasted_iota(jnp.int32, (r * KV_SUB, 1), 0) & 1) == 0
    out = jnp.where(is_key, _rms_rows(outs[0], gkc_ref[...]), outs[1])
    return out.reshape(r, KV_SUB, HEAD_DIM)


def _compress_prompt_kernel(kv_ref, logit_ref, w1_ref, w2_ref, gkc_ref, o_ref):
    a1, a2 = _pool_weights(logit_ref)
    p1, p2 = _pool(kv_ref[...], a1, a2)
    pooled = p1 + jnp.concatenate([p2[1:], p2[:1]], axis=0)
    o_ref[...] = _cmp_mlp(pooled, w1_ref, w2_ref, gkc_ref)


def _compress_prompt(kv, logits_x, w1, w2, g_kc):
    nblk = SEQ // CMP_STRIDE
    cst = lambda b: (0, 0)
    cst3 = lambda b: (0, 0, 0)
    return pl.pallas_call(
        _compress_prompt_kernel,
        grid=(BATCH,),
        in_specs=[pl.BlockSpec((None, SEQ, KV_SUB, HEAD_DIM), lambda b: (b, 0, 0, 0)),
                  pl.BlockSpec((CMP_LEN, KV_SUB, HEAD_DIM), cst3),
                  pl.BlockSpec(w1.shape, cst3), pl.BlockSpec(w2.shape, cst3),
                  pl.BlockSpec((1, HEAD_DIM), cst)],
        out_specs=pl.BlockSpec((None, nblk, KV_SUB, HEAD_DIM), lambda b: (b, 0, 0, 0)),
        out_shape=jax.ShapeDtypeStruct((BATCH, nblk, KV_SUB, HEAD_DIM), F32),
        compiler_params=_cparams(("arbitrary",)),
        name="compress_prompt",
    )(kv, logits_x, w1, w2, g_kc.reshape(1, HEAD_DIM))


PAGES_PER_STEP = 8


def _compress_sample_kernel(pt_ref, *refs):
    pages = refs[:PAGES_PER_STEP]
    nxt_ref, new_ref, logit_ref, w1_ref, w2_ref, gkc_ref, o_ref = refs[PAGES_PER_STEP:]
    a1, a2 = _pool_weights(logit_ref)
    last = pl.program_id(1) == pl.num_programs(1) - 1
    nxt = jnp.where(last, new_ref[...], nxt_ref[...])
    tail = jnp.sum(nxt * a2, axis=0, keepdims=True)
    pooled = [None] * PAGES_PER_STEP
    for kk in reversed(range(PAGES_PER_STEP)):
        p1, p2 = _pool(pages[kk][...], a1, a2)
        pooled[kk] = p1 + jnp.concatenate([p2[1:], tail], axis=0)
        tail = p2[:1]
    o_ref[...] = _cmp_mlp(jnp.concatenate(pooled, axis=0), w1_ref, w2_ref, gkc_ref)


def _compress_sample(cache, pt_flat, new_rows, logits_x, w1, w2, g_kc):
    steps = N_PAGES // PAGES_PER_STEP
    rows = PAGES_PER_STEP * PAGE_SIZE // CMP_STRIDE
    cst = lambda b, s, pt: (0, 0)
    cst3 = lambda b, s, pt: (0, 0, 0)

    def page_spec(kk):
        return pl.BlockSpec((None, PAGE_SIZE, KV_SUB, HEAD_DIM),
                            lambda b, s, pt: (pt[b * N_PAGES + s * PAGES_PER_STEP + kk], 0, 0, 0))

    nxt_spec = pl.BlockSpec(
        (None, CMP_STRIDE, KV_SUB, HEAD_DIM),
        lambda b, s, pt: (pt[b * N_PAGES + jnp.minimum((s + 1) * PAGES_PER_STEP, N_PAGES - 1)], 0, 0, 0))
    new_spec = pl.BlockSpec((None, CMP_STRIDE, KV_SUB, HEAD_DIM), lambda b, s, pt: (b, 0, 0, 0))
    nc = PAST_LEN // CMP_STRIDE
    return pl.pallas_call(
        _compress_sample_kernel,
        grid_spec=pltpu.PrefetchScalarGridSpec(
            num_scalar_prefetch=1,
            grid=(DEC_BATCH, steps),
            in_specs=[page_spec(kk) for kk in range(PAGES_PER_STEP)] + [
                nxt_spec, new_spec, pl.BlockSpec((CMP_LEN, KV_SUB, HEAD_DIM), cst3),
                pl.BlockSpec(w1.shape, cst3), pl.BlockSpec(w2.shape, cst3), pl.BlockSpec((1, HEAD_DIM), cst)],
            out_specs=pl.BlockSpec((None, rows, KV_SUB, HEAD_DIM), lambda b, s, pt: (b, s, 0, 0)),
        ),
        out_shape=jax.ShapeDtypeStruct((DEC_BATCH, nc, KV_SUB, HEAD_DIM), F32),
        compiler_params=_cparams(("arbitrary", "arbitrary")),
        name="compress_sample",
    )(pt_flat, *([cache] * (PAGES_PER_STEP + 1)), new_rows, logits_x, w1, w2, g_kc.reshape(1, HEAD_DIM))


def _overlap_matrix(ncp, nbp, nb):
    c0 = np.arange(ncp)[:, None] * CMP_STRIDE
    b0 = np.arange(nbp)[None, :] * SEL_BLOCK
    ov = (c0 < b0 + SEL_BLOCK) & (c0 + CMP_LEN > b0) & (np.arange(nbp)[None, :] < nb)
    return ov.astype(np.float32)


def _cmp_select_kernel(q_ref, ckv_ref, ov_ref, o_ref, m_ref, *maybe_idx, tq, nb, pos_base, tiled):
    ncp = ckv_ref.shape[0]
    nbp = ov_ref.shape[1]
    t0 = pos_base + (pl.program_id(1) * tq if tiled else 0)
    qp = t0 + lax.broadcasted_iota(jnp.int32, (tq, 1), 0)
    cend = lax.broadcasted_iota(jnp.int32, (1, ncp), 1) * CMP_STRIDE + (CMP_LEN - 1)
    valid = qp >= cend
    validf = valid.astype(F32)
    dist = (qp - cend).astype(F32)
    n_i = lax.broadcasted_iota(jnp.int32, (1, nbp), 1)
    qblk = lax.shift_right_arithmetic(qp, int(math.log2(SEL_BLOCK)))
    forced = (n_i == 0) | (n_i == qblk) | (n_i == qblk - 1)
    future = n_i * SEL_BLOCK > qp
    lane = lax.broadcasted_iota(jnp.int32, (1, HEAD_DIM), 1)
    for g in range(NSA_KV_HEADS):
        gs = slice(g * HEAD_DIM, (g + 1) * HEAD_DIM)
        kcg = ckv_ref[:, 2 * g * HEAD_DIM:(2 * g + 1) * HEAD_DIM].astype(BF16)
        vcg = ckv_ref[:, (2 * g + 1) * HEAD_DIM:(2 * g + 2) * HEAD_DIM].astype(BF16)
        q4 = jnp.concatenate([q_ref[:, (g * NSA_REP + r) * HEAD_DIM:(g * NSA_REP + r + 1) * HEAD_DIM]
                              for r in range(NSA_REP)], axis=0).astype(BF16)
        s4 = _dot_nt(q4, kcg) * SCALE
        psum = jnp.zeros((tq, ncp), F32)
        for r in range(NSA_REP):
            h = g * NSA_REP + r
            s = jnp.where(valid, s4[r * tq:(r + 1) * tq] - SLOPES[h] * dist, NEG)
            e = jnp.exp(s - jnp.max(s, axis=-1, keepdims=True))
            p = e / jnp.sum(e, axis=-1, keepdims=True) * validf
            o_ref[:, h * HEAD_DIM:(h + 1) * HEAD_DIM] = _dot(p.astype(BF16), vcg)
            psum = psum + p
        ov = ov_ref[...]
        imp = sum(_dot(part, ov) for part in _split3(psum))
        score = jnp.where(forced, FORCE, jnp.where(future, -1.0, imp))
        score = jnp.where(n_i < nb, score, -2.0)
        rank = jnp.zeros((tq, nbp), F32)
        for m in range(nb):
            col = score[:, m:m + 1]
            ahead = (col > score) | ((col == score) & (n_i > m))
            rank = rank + jnp.where(ahead, 1.0, 0.0)
        chosen = (rank < float(N_SEL)) & (n_i < nb) & jnp.logical_not(future)
        m_ref[:, g * nbp:(g + 1) * nbp] = chosen.astype(F32)
        if maybe_idx:
            nf = n_i.astype(F32)
            idx = jnp.zeros((tq, HEAD_DIM), F32)
            for jj in range(N_SEL):
                pick = jnp.sum(jnp.where(rank == float(jj), nf, 0.0), axis=-1, keepdims=True)
                idx = jnp.where(lane == jj, pick, idx)
            maybe_idx[0][:, gs] = idx


def _cmp_select(z, ckv, *, batch, rows, row0, tq, nb, pos_base, want_idx):
    ncp = ckv.shape[1]
    nbp = -(-nb // 128) * 128
    ov = jnp.asarray(_overlap_matrix(ncp, nbp, nb), BF16)
    n_q = rows // tq
    blk0 = row0 // tq
    rowmap = lambda b, i: (blk0 + b * n_q + i, 0)
    out_specs = [pl.BlockSpec((tq, NSA_WIDTH), lambda b, i: (b * n_q + i, 0)),
                 pl.BlockSpec((tq, NSA_KV_HEADS * nbp), lambda b, i: (b * n_q + i, 0))]
    out_shape = [jax.ShapeDtypeStruct((batch * rows, NSA_WIDTH), F32),
                 jax.ShapeDtypeStruct((batch * rows, NSA_KV_HEADS * nbp), F32)]
    if want_idx:
        out_specs.append(pl.BlockSpec((tq, NSA_KV_HEADS * HEAD_DIM), lambda b, i: (b * n_q + i, 0)))
        out_shape.append(jax.ShapeDtypeStruct((batch * rows, NSA_KV_HEADS * HEAD_DIM), F32))
    return pl.pallas_call(
        functools.partial(_cmp_select_kernel, tq=tq, nb=nb, pos_base=pos_base, tiled=n_q > 1),
        grid=(batch, n_q),
        in_specs=[pl.BlockSpec((tq, NSA_WIDTH), lambda b, i: (blk0 + b * n_q + i, COL_NQ // NSA_WIDTH)),
                  pl.BlockSpec((None, ncp, KV_ROW), lambda b, i: (b, 0, 0)),
                  pl.BlockSpec(ov.shape, lambda b, i: (0, 0))],
        out_specs=out_specs,
        out_shape=out_shape,
        compiler_params=_cparams(("arbitrary", "arbitrary")),
        name="cmp_select",
    )(z, ckv, ov)


FLASH_TQ = 256
FLASH_TK = 256


def _flash_kernel(slope_ref, q_ref, kv_ref, *rest, mode):
    if mode == "sel":
        mask_ref, exp_ref, o_ref, m_scr, l_scr, acc_scr = rest
    else:
        o_ref, m_scr, l_scr, acc_scr = rest
    tq, tk = FLASH_TQ, FLASH_TK
    g = pl.program_id(1)
    qi = pl.program_id(2)
    kj = pl.program_id(3)
    if mode == "sel":
        ka = kj
        needed = kj * tk <= qi * tq + (tq - 1)
    else:
        ka = qi * (tq // tk) - WINDOW // tk + kj
        needed = ka >= 0

    @pl.when(kj == 0)
    def _():
        m_scr[...] = jnp.full_like(m_scr, NEG)
        l_scr[...] = jnp.zeros_like(l_scr)
        acc_scr[...] = jnp.zeros_like(acc_scr)

    @pl.when(needed)
    def _():
        q4 = jnp.concatenate([q_ref[:, r * HEAD_DIM:(r + 1) * HEAD_DIM] for r in range(NSA_REP)],
                             axis=0).astype(BF16)
        kb = kv_ref[:, :HEAD_DIM].astype(BF16)
        vb = kv_ref[:, HEAD_DIM:].astype(BF16)
        s4 = _dot_nt(q4, kb) * SCALE
        qpos = qi * tq + lax.broadcasted_iota(jnp.int32, (tq, 1), 0)
        kpos = ka * tk + lax.broadcasted_iota(jnp.int32, (1, tk), 1)
        d = qpos - kpos
        ok = d >= 0
        if mode == "sel":
            ok = ok & (_dot(mask_ref[...].astype(BF16), exp_ref[...]) > 0.5)
        else:
            ok = ok & (d < WINDOW)
        df = d.astype(F32)
        for r in range(NSA_REP):
            s = jnp.where(ok, s4[r * tq:(r + 1) * tq] - slope_ref[g * NSA_REP + r] * df, NEG)
            m_prev = m_scr[r]
            m_new = jnp.maximum(m_prev, jnp.max(s, axis=-1, keepdims=True))
            p = jnp.where(ok, jnp.exp(s - m_new), 0.0)
            alpha = jnp.exp(m_prev - m_new)
            l_scr[r] = alpha * l_scr[r] + jnp.sum(p, axis=-1, keepdims=True)
            acc_scr[r] = alpha * acc_scr[r] + _dot(p.astype(BF16), vb)
            m_scr[r] = m_new

    @pl.when(kj == pl.num_programs(3) - 1)
    def _():
        for r in range(NSA_REP):
            o_ref[:, r * HEAD_DIM:(r + 1) * HEAD_DIM] = acc_scr[r] / l_scr[r]


def _flash_prompt(z, mode, mask=None):
    tq, tk = FLASH_TQ, FLASH_TK
    n_q = SEQ // tq
    slopes = jnp.asarray(SLOPES, F32)
    qw = NSA_REP * HEAD_DIM
    kvw = 2 * HEAD_DIM
    if mode == "sel":
        n_kv = SEQ // tk
        col0 = COL_KVS // kvw
        kv_map = lambda b, g, i, j: (b * n_kv + jnp.minimum(j, (i * tq + tq - 1) // tk), col0 + g)
    else:
        n_kv = WINDOW // tk + tq // tk
        col0 = COL_KVW // kvw
        kv_map = lambda b, g, i, j: (b * (SEQ // tk) + jnp.maximum(i * (tq // tk) - WINDOW // tk + j, 0), col0 + g)
    in_specs = [pl.BlockSpec(memory_space=pltpu.SMEM),
                pl.BlockSpec((tq, qw), lambda b, g, i, j: (b * n_q + i, COL_NQ // qw + g)),
                pl.BlockSpec((tk, kvw), kv_map)]
    args = [slopes, z, z]
    if mode == "sel":
        nbp = mask.shape[1] // NSA_KV_HEADS
        blk = (np.arange(SEQ)[None, :] // SEL_BLOCK) == np.arange(nbp)[:, None]
        expand = jnp.asarray(blk.reshape(nbp, n_kv, tk).transpose(1, 0, 2).astype(np.float32), BF16)
        in_specs += [pl.BlockSpec((tq, nbp), lambda b, g, i, j: (b * n_q + i, g)),
                     pl.BlockSpec((None, nbp, tk), lambda b, g, i, j: (jnp.minimum(j, (i * tq + tq - 1) // tk), 0, 0))]
        args += [mask, expand]
    return pl.pallas_call(
        functools.partial(_flash_kernel, mode=mode),
        grid=(BATCH, NSA_KV_HEADS, n_q, n_kv),
        in_specs=in_specs,
        out_specs=pl.BlockSpec((tq, qw), lambda b, g, i, j: (b * n_q + i, g)),
        out_shape=jax.ShapeDtypeStruct((NP, NSA_WIDTH), F32),
        scratch_shapes=[pltpu.VMEM((NSA_REP, tq, 1), F32), pltpu.VMEM((NSA_REP, tq, 1), F32),
                        pltpu.VMEM((NSA_REP, tq, HEAD_DIM), F32)],
        compiler_params=_cparams(("arbitrary",) * 4),
        name="flash_" + mode,
    )(*args)


def _softmax_parts(scores, oks):
    mx = functools.reduce(jnp.maximum, [jnp.max(jnp.where(ok, s, NEG), axis=-1, keepdims=True)
                                        for s, ok in zip(scores, oks)])
    es = [jnp.where(ok, jnp.exp(jnp.where(ok, s, NEG) - mx), 0.0) for s, ok in zip(scores, oks)]
    den = functools.reduce(lambda a, b: a + b, [jnp.sum(e, axis=-1, keepdims=True) for e in es])
    return [e / den for e in es]


def _win_sample_kernel(q_ref, cache_ref, new_ref, o_ref):
    wk = cache_ref.shape[0]
    row = lax.broadcasted_iota(jnp.int32, (NSA_REP * SROW, 1), 0)
    qi = row & (SROW - 1)
    jc = lax.broadcasted_iota(jnp.int32, (1, wk), 1)
    jn = lax.broadcasted_iota(jnp.int32, (1, SROW), 1)
    d_c = qi + (wk - jc)
    d_n = qi - jn
    ok_c = d_c < WINDOW
    ok_n = (d_n >= 0) & (jn < DEC_SEQ)
    for g in range(NSA_KV_HEADS):
        kcol = slice(g * 2 * HEAD_DIM, g * 2 * HEAD_DIM + HEAD_DIM)
        vcol = slice(g * 2 * HEAD_DIM + HEAD_DIM, (g + 1) * 2 * HEAD_DIM)
        q4 = jnp.concatenate([q_ref[:, (g * NSA_REP + r) * HEAD_DIM:(g * NSA_REP + r + 1) * HEAD_DIM]
                              for r in range(NSA_REP)], axis=0).astype(BF16)
        s_c = _dot_nt(q4, cache_ref[:, kcol].astype(BF16)) * SCALE
        s_n = _dot_nt(q4, new_ref[:, kcol].astype(BF16)) * SCALE
        slope = jnp.concatenate([jnp.full((SROW, 1), SLOPES[g * NSA_REP + r], F32) for r in range(NSA_REP)], axis=0)
        s_c = s_c - slope * d_c.astype(F32)
        s_n = s_n - slope * d_n.astype(F32)
        p_c, p_n = _softmax_parts([s_c, s_n], [ok_c, ok_n])
        o = _dot(p_c.astype(BF16), cache_ref[:, vcol].astype(BF16)) + _dot(p_n.astype(BF16), new_ref[:, vcol].astype(BF16))
        for r in range(NSA_REP):
            h = g * NSA_REP + r
            o_ref[:, h * HEAD_DIM:(h + 1) * HEAD_DIM] = o[r * SROW:(r + 1) * SROW]


def _win_sample(z, cache_win):
    wk = cache_win.shape[1]
    return pl.pallas_call(
        _win_sample_kernel,
        grid=(DEC_BATCH,),
        in_specs=[pl.BlockSpec((SROW, NSA_WIDTH), lambda b: (NP // SROW + b, COL_NQ // NSA_WIDTH)),
                  pl.BlockSpec((None, wk, KV_ROW), lambda b: (b, 0, 0)),
                  pl.BlockSpec((SROW, KV_ROW), lambda b: (NP // SROW + b, COL_KVW // KV_ROW))],
        out_specs=pl.BlockSpec((SROW, NSA_WIDTH), lambda b: (b, 0)),
        out_shape=jax.ShapeDtypeStruct((NS, NSA_WIDTH), F32),
        compiler_params=_cparams(("arbitrary",)),
        name="win_sample",
    )(z, cache_win, z)


N_GATHER = N_SEL - 1


def _sel_sample_kernel(phys_ref, blk_ref, slope_ref, q_ref, new_ref, *rest):
    blocks = rest[:N_GATHER]
    o_ref = rest[N_GATHER]
    b = pl.program_id(0)
    g = pl.program_id(1)
    i = pl.program_id(2)

    @pl.when(i == 0)
    def _():
        o_ref[...] = jnp.zeros_like(o_ref)

    base = ((b * DEC_SEQ + i) * NSA_KV_HEADS + g) * N_GATHER
    rr = lax.broadcasted_iota(jnp.int32, (8, 1), 0)
    mine = rr == i
    q4 = jnp.concatenate([jnp.sum(jnp.where(mine, q_ref[0:8, r * HEAD_DIM:(r + 1) * HEAD_DIM], 0.0),
                                  axis=0, keepdims=True) for r in range(NSA_REP)]
                         + [jnp.zeros((8 - NSA_REP, HEAD_DIM), F32)], axis=0).astype(BF16)
    slope = jnp.zeros((8, 1), F32)
    for r in range(NSA_REP):
        slope = jnp.where(rr == r, slope_ref[g * NSA_REP + r], slope)
    qp = PAST_LEN + i
    off = lax.broadcasted_iota(jnp.int32, (1, SEL_BLOCK), 1)
    scores, oks = [], []
    for jj in range(N_GATHER):
        d = qp - (blk_ref[base + jj] * SEL_BLOCK + off)
        s = _dot_nt(q4, blocks[jj][:, :HEAD_DIM].astype(BF16)) * SCALE - slope * d.astype(F32)
        scores.append(s)
        oks.append(jnp.broadcast_to(d >= 0, s.shape))
    jn = lax.broadcasted_iota(jnp.int32, (1, SROW), 1)
    d_n = i - jn
    scores.append(_dot_nt(q4, new_ref[:, :HEAD_DIM].astype(BF16)) * SCALE - slope * d_n.astype(F32))
    oks.append(jnp.broadcast_to((d_n >= 0) & (jn < DEC_SEQ), scores[-1].shape))
    ps = _softmax_parts(scores, oks)
    o = _dot(ps[-1].astype(BF16), new_ref[:, HEAD_DIM:].astype(BF16))
    for jj in range(N_GATHER):
        o = o + _dot(ps[jj].astype(BF16), blocks[jj][:, HEAD_DIM:].astype(BF16))
    for r in range(NSA_REP):
        cs = slice(r * HEAD_DIM, (r + 1) * HEAD_DIM)
        o_ref[0:8, cs] = jnp.where(mine, o[r:r + 1], o_ref[0:8, cs])


def _sel_sample(z, cache_sel, phys_half, blk_no):
    kvw = 2 * HEAD_DIM
    qw = NSA_REP * HEAD_DIM
    halves = cache_sel.reshape(-1, SEL_BLOCK, KV_ROW)

    def blk_spec(jj):
        return pl.BlockSpec(
            (None, SEL_BLOCK, kvw),
            lambda b, g, i, ph, bn: (ph[((b * DEC_SEQ + i) * NSA_KV_HEADS + g) * N_GATHER + jj], 0, g))

    return pl.pallas_call(
        _sel_sample_kernel,
        grid_spec=pltpu.PrefetchScalarGridSpec(
            num_scalar_prefetch=2,
            grid=(DEC_BATCH, NSA_KV_HEADS, DEC_SEQ),
            in_specs=[pl.BlockSpec(memory_space=pltpu.SMEM),
                      pl.BlockSpec((SROW, qw), lambda b, g, i, ph, bn: (NP // SROW + b, COL_NQ // qw + g)),
                      pl.BlockSpec((SROW, kvw), lambda b, g, i, ph, bn: (NP // SROW + b, COL_KVS // kvw + g))]
            + [blk_spec(jj) for jj in range(N_GATHER)],
            out_specs=pl.BlockSpec((SROW, qw), lambda b, g, i, ph, bn: (b, g)),
        ),
        out_shape=jax.ShapeDtypeStruct((NS, NSA_WIDTH), F32),
        compiler_params=_cparams(("arbitrary",) * 3),
        name="sel_sample",
    )(phys_half, blk_no, jnp.asarray(SLOPES, F32), z, z, *([halves] * N_GATHER))


def _mix_kernel(ylin_ref, cmp_ref, sel_ref, win_ref, gate_ref, gn_ref, y_ref):
    gates = gate_ref[...]
    parts = []
    for h in range(NSA_HEADS):
        hs = slice(h * HEAD_DIM, (h + 1) * HEAD_DIM)
        parts.append(gates[:, h:h + 1] * cmp_ref[:, hs]
                     + gates[:, NSA_HEADS + h:NSA_HEADS + h + 1] * sel_ref[:, hs]
                     + gates[:, 2 * NSA_HEADS + h:2 * NSA_HEADS + h + 1] * win_ref[:, hs])
    o = jnp.concatenate(parts, axis=1)
    y_ref[:, :LIN_WIDTH] = ylin_ref[...].astype(BF16)
    y_ref[:, LIN_WIDTH:] = _rms_rows(o, gn_ref[...]).astype(BF16)


def _mix(ylin, o_cmp, o_sel, o_win, gates, g_nsa_out, tm):
    row = lambda i: (i, 0)
    return pl.pallas_call(
        _mix_kernel,
        grid=(NT // tm,),
        in_specs=[pl.BlockSpec((tm, LIN_WIDTH), row)] + [pl.BlockSpec((tm, NSA_WIDTH), row)] * 3
        + [pl.BlockSpec((tm, 128), row), pl.BlockSpec((1, NSA_WIDTH), lambda i: (0, 0))],
        out_specs=pl.BlockSpec((tm, D_MODEL), row),
        out_shape=jax.ShapeDtypeStruct((NT, D_MODEL), BF16),
        compiler_params=_cparams(("arbitrary",)),
        name="mix",
    )(ylin, o_cmp, o_sel, o_win, gates, g_nsa_out.reshape(1, NSA_WIDTH))


def _router_kernel(h_ref, g_ref, wr_ref, br_ref, m_ref, r_ref):
    m = _rms_rows(h_ref[...], g_ref[...])
    m_ref[...] = m
    logits = jnp.dot(m, wr_ref[...], precision=lax.Precision.HIGHEST, preferred_element_type=F32) + br_ref[...]
    lane = lax.broadcasted_iota(jnp.int32, (1, 128), 1)
    big = 1024
    is_g = lane < N_GROUPS
    gl = jnp.where(is_g, logits, -jnp.inf)
    gmax = jnp.max(gl, axis=-1, keepdims=True)
    gsel = jnp.min(jnp.where(gl == gmax, lane, big), axis=-1, keepdims=True)
    gw = 1.0 / jnp.sum(jnp.where(is_g, jnp.exp(gl - gmax), 0.0), axis=-1, keepdims=True)
    e_id = lane - N_GROUPS
    in_grp = (e_id >= 0) & (e_id < N_EXPERTS) & (lax.shift_right_arithmetic(e_id, int(math.log2(EXP_PER_GROUP))) == gsel)
    el = jnp.where(in_grp, logits, -jnp.inf)
    v1 = jnp.max(el, axis=-1, keepdims=True)
    i1 = jnp.min(jnp.where(el == v1, lane, big), axis=-1, keepdims=True)
    el2 = jnp.where(lane == i1, -jnp.inf, el)
    v2 = jnp.max(el2, axis=-1, keepdims=True)
    i2 = jnp.min(jnp.where(el2 == v2, lane, big), axis=-1, keepdims=True)
    t = jnp.exp(v2 - v1)
    w1 = gw / (1.0 + t)
    w2 = gw * t / (1.0 + t)
    out = jnp.where(lane == 0, (i1 - N_GROUPS).astype(F32), 0.0)
    out = jnp.where(lane == 1, (i2 - N_GROUPS).astype(F32), out)
    out = jnp.where(lane == 2, w1, out)
    out = jnp.where(lane == 3, w2, out)
    r_ref[...] = out


def _router(h, g_ffn, w_route, b_route, tm):
    row = lambda i: (i, 0)
    cst = lambda i: (0, 0)
    return pl.pallas_call(
        _router_kernel,
        grid=(NT // tm,),
        in_specs=[pl.BlockSpec((tm, D_MODEL), row), pl.BlockSpec((1, D_MODEL), cst),
                  pl.BlockSpec((D_MODEL, 128), cst), pl.BlockSpec((1, 128), cst)],
        out_specs=[pl.BlockSpec((tm, D_MODEL), row), pl.BlockSpec((tm, 128), row)],
        out_shape=[jax.ShapeDtypeStruct((NT, D_MODEL), F32), jax.ShapeDtypeStruct((NT, 128), F32)],
        compiler_params=_cparams(("arbitrary",)),
        name="moe_router",
    )(h, g_ffn.reshape(1, D_MODEL), w_route, b_route)


def _row_gather_kernel(idx_ref, src_ref, o_ref, buf, sem, *, rows):
    base = pl.program_id(0) * rows

    def copy(r):
        return pltpu.make_async_copy(src_ref.at[pl.ds(idx_ref[base + r], 1), :], buf.at[pl.ds(r, 1), :], sem)

    def start(r, c):
        copy(r).start()
        return c

    def wait(r, c):
        copy(r).wait()
        return c

    lax.fori_loop(0, rows, start, 0)
    lax.fori_loop(0, rows, wait, 0)
    o_ref[...] = buf[...].astype(o_ref.dtype)


def _row_gather(src, idx, rows, dtype):
    n = idx.shape[0]
    d = src.shape[1]
    return pl.pallas_call(
        functools.partial(_row_gather_kernel, rows=rows),
        grid_spec=pltpu.PrefetchScalarGridSpec(
            num_scalar_prefetch=1,
            grid=(n // rows,),
            in_specs=[pl.BlockSpec(memory_space=pl.ANY)],
            out_specs=pl.BlockSpec((rows, d), lambda i, ix: (i, 0)),
            scratch_shapes=[pltpu.VMEM((rows, d), src.dtype), pltpu.SemaphoreType.DMA(())],
        ),
        out_shape=jax.ShapeDtypeStruct((n, d), dtype),
        compiler_params=_cparams(("arbitrary",)),
        name="row_gather",
    )(idx, src)


EXPERT_TM = 512
EXPERT_TF = 128
EXPERT_TN = 1024
N_ASSIGN = 2 * NT
EXPERT_TILES = N_ASSIGN // EXPERT_TM + N_EXPERTS
R_PAD = EXPERT_TILES * EXPERT_TM


def _expert_kernel(te_ref, tv_ref, xs_ref, wg_ref, wu_ref, wd_ref, rw_ref, o_ref):
    i = pl.program_id(0)
    j = pl.program_id(1)
    live = tv_ref[i] > 0

    @pl.when(live)
    def _():
        x = xs_ref[...]
        gate = _dot(x, wg_ref[...].astype(BF16))
        up = _dot(x, wu_ref[...].astype(BF16))
        hid = _silu(gate) * up * rw_ref[...]
        hb = hid.astype(BF16)

        @pl.when(j == 0)
        def _():
            o_ref[...] = jnp.zeros_like(o_ref)

        for c in range(D_MODEL // EXPERT_TN):
            cs = slice(c * EXPERT_TN, (c + 1) * EXPERT_TN)
            o_ref[:, cs] += _dot(hb, wd_ref[:, cs].astype(BF16))

    @pl.when(jnp.logical_not(live) & (j == 0))
    def _():
        o_ref[...] = jnp.zeros_like(o_ref)


def _experts(xs, roww, tile_expert, tile_live, w_g, w_u, w_d):
    tm, tf = EXPERT_TM, EXPERT_TF
    return pl.pallas_call(
        _expert_kernel,
        grid_spec=pltpu.PrefetchScalarGridSpec(
            num_scalar_prefetch=2,
            grid=(EXPERT_TILES, D_EXPERT // tf),
            in_specs=[pl.BlockSpec((tm, D_MODEL), lambda i, j, te, tv: (i, 0)),
                      pl.BlockSpec((None, D_MODEL, tf), lambda i, j, te, tv: (te[i], 0, j)),
                      pl.BlockSpec((None, D_MODEL, tf), lambda i, j, te, tv: (te[i], 0, j)),
                      pl.BlockSpec((None, tf, D_MODEL), lambda i, j, te, tv: (te[i], j, 0)),
                      pl.BlockSpec((tm, 1), lambda i, j, te, tv: (i, 0))],
            out_specs=pl.BlockSpec((tm, D_MODEL), lambda i, j, te, tv: (i, 0)),
        ),
        out_shape=jax.ShapeDtypeStruct((R_PAD, D_MODEL), F32),
        compiler_params=_cparams(("arbitrary", "arbitrary")),
        name="moe_experts",
    )(tile_expert, tile_live, xs, w_g, w_u, w_d, roww)


def _combine_kernel(p0_ref, p1_ref, ys_ref, h_ref, g_ref, h2_ref, n_ref, buf0, buf1, sem, *, rows):
    base = pl.program_id(0) * rows

    def copies(r):
        return (pltpu.make_async_copy(ys_ref.at[pl.ds(p0_ref[base + r], 1), :], buf0.at[pl.ds(r, 1), :], sem.at[0]),
                pltpu.make_async_copy(ys_ref.at[pl.ds(p1_ref[base + r], 1), :], buf1.at[pl.ds(r, 1), :], sem.at[1]))

    def start(r, c):
        for cp in copies(r):
            cp.start()
        return c

    def wait(r, c):
        for cp in copies(r):
            cp.wait()
        return c

    lax.fori_loop(0, rows, start, 0)
    lax.fori_loop(0, rows, wait, 0)
    h2 = h_ref[...] + (buf0[...] + buf1[...])
    h2_ref[...] = h2
    n_ref[...] = _rms_rows(h2, g_ref[...]).astype(BF16)


def _combine(ys, pos0, pos1, h, g_ple_in, rows):
    row = lambda i, a, b: (i, 0)
    return pl.pallas_call(
        functools.partial(_combine_kernel, rows=rows),
        grid_spec=pltpu.PrefetchScalarGridSpec(
            num_scalar_prefetch=2,
            grid=(NT // rows,),
            in_specs=[pl.BlockSpec(memory_space=pl.ANY), pl.BlockSpec((rows, D_MODEL), row),
                      pl.BlockSpec((1, D_MODEL), lambda i, a, b: (0, 0))],
            out_specs=[pl.BlockSpec((rows, D_MODEL), row), pl.BlockSpec((rows, D_MODEL), row)],
            scratch_shapes=[pltpu.VMEM((rows, D_MODEL), F32), pltpu.VMEM((rows, D_MODEL), F32),
                            pltpu.SemaphoreType.DMA((2,))],
        ),
        out_shape=[jax.ShapeDtypeStruct((NT, D_MODEL), F32), jax.ShapeDtypeStruct((NT, D_MODEL), BF16)],
        compiler_params=_cparams(("arbitrary",)),
        name="moe_combine",
    )(pos0, pos1, ys, h, g_ple_in.reshape(1, D_MODEL))


def _dispatch_plan(route):
    tm = EXPERT_TM
    a_exp = route[:, 0:2].astype(jnp.int32).reshape(-1)
    a_w = route[:, 2:4].reshape(-1)
    a_tok = jnp.arange(N_ASSIGN, dtype=jnp.int32) // 2
    order = jnp.argsort(a_exp, stable=True)
    counts = jnp.bincount(a_exp, length=N_EXPERTS).astype(jnp.int32)
    padded = -(-counts // tm) * tm
    pad_end = jnp.cumsum(padded)
    pad_start = pad_end - padded
    start = jnp.cumsum(counts) - counts
    sorted_rank = jnp.zeros((N_ASSIGN,), jnp.int32).at[order].set(jnp.arange(N_ASSIGN, dtype=jnp.int32))
    pos = pad_start[a_exp] + sorted_rank - start[a_exp]
    src = jnp.zeros((R_PAD,), jnp.int32).at[pos].set(a_tok)
    roww = jnp.zeros((R_PAD,), F32).at[pos].set(a_w)
    tile_row = jnp.arange(EXPERT_TILES, dtype=jnp.int32) * tm
    tile_live = (tile_row < pad_end[-1]).astype(jnp.int32)
    tile_expert = jnp.minimum(jnp.searchsorted(pad_end, tile_row, side="right"), N_EXPERTS - 1).astype(jnp.int32)
    last_live = jnp.maximum(pad_end[-1] // tm - 1, 0)
    tile_expert = jnp.where(tile_live > 0, tile_expert, tile_expert[last_live])
    pos2 = pos.reshape(NT, 2)
    return src, roww.reshape(R_PAD, 1), tile_expert, tile_live, pos2[:, 0], pos2[:, 1]


def _rows_of(prompt, sample):
    d = prompt.shape[-1]
    s = jnp.pad(sample, ((0, 0), (0, SROW - DEC_SEQ), (0, 0)))
    return jnp.concatenate([prompt.reshape(NP, d), s.reshape(NS, d)], axis=0)


def _kv_rows(z, col):
    blk = z[:, col:col + KV_ROW]
    prompt = blk[:NP].reshape(1, BATCH, SEQ, NSA_KV_HEADS, 2, HEAD_DIM)
    sample = blk[NP:].reshape(DEC_BATCH, SROW, NSA_KV_HEADS, 2, HEAD_DIM)[None, :, :DEC_SEQ]
    return prompt, sample


def kernel(x_prompt, x_sample, cache_cmp, cache_sel, cache_win, state_hgrn, page_table, p_prompt, p_sample, g_attn, w_in, lb_logits, g_lin_out, g_q, g_kc, g_ks, g_kw, cmp_pos_logits, cmp_w1, cmp_w2, g_nsa_out, w_out, g_ffn, w_group_router, b_group_router, w_expert_router, b_expert_router, w_exp_gate, w_exp_up, w_exp_down, g_ple_in, w_ple_gate, w_ple_proj, g_ple_out):
    l = 0
    x = _rows_of(x_prompt, x_sample)
    p_rows = _rows_of(p_prompt[l], p_sample[l])

    a = _rms_cast(x, g_attn[l], 544)
    tn = 512
    ones = jnp.ones((HEAD_DIM,), F32)
    kv_gain = lambda gk: jnp.tile(jnp.concatenate([gk, ones]), NSA_KV_HEADS)
    kv_sel = jnp.tile(jnp.concatenate([ones, 0 * ones]), NSA_KV_HEADS)
    gain = jnp.concatenate([jnp.ones((COL_NQ,), F32), jnp.tile(g_q[l], NSA_HEADS), jnp.ones((KV_ROW,), F32),
                            kv_gain(g_ks[l]), kv_gain(g_kw[l])]).reshape(1, N_MAIN)
    normsel = jnp.concatenate([jnp.zeros((COL_NQ,), F32), jnp.ones((NSA_WIDTH,), F32), jnp.zeros((KV_ROW,), F32),
                               kv_sel, kv_sel]).reshape(1, N_MAIN)
    tile_norm = np.zeros((N_MAIN // tn,), np.int32)
    tile_norm[COL_NQ // tn:COL_KVC // tn] = 1
    tile_norm[COL_KVS // tn:] = 1
    z = _matmul(a, w_in[l], N_MAIN, 1088, tn, "groupnorm", (gain, normsel), ("row", "row"),
                flags=jnp.asarray(tile_norm), name="in_proj")
    w_gate = jnp.pad(w_in[l][:, COL_GATE:COL_GATE + N_GATE], ((0, 0), (0, 128 - N_GATE)))
    gates = _matmul(a, w_gate, 128, 1088, 128, "sigmoid", name="gate_proj")

    lb = jnp.cumsum(jax.nn.softmax(lb_logits.astype(F32), axis=0), axis=0)[l]
    ylin_p, st_p = _hgrn(z, lb, g_lin_out[l], None, batch=BATCH, rows=SEQ, row0=0, c=128, t_valid=128)
    ylin_s, st_s = _hgrn(z, lb, g_lin_out[l], state_hgrn[l], batch=DEC_BATCH, rows=SROW, row0=NP, c=SROW,
                         t_valid=DEC_SEQ)

    cmp_p, cmp_s = _kv_rows(z, COL_KVC)
    sel_p, sel_s = _kv_rows(z, COL_KVS)
    win_p, win_new = _kv_rows(z, COL_KVW)
    logits_x = jnp.broadcast_to(cmp_pos_logits[l].transpose(2, 0, 1).reshape(CMP_LEN, KV_SUB, 1),
                                (CMP_LEN, KV_SUB, HEAD_DIM))
    pt_flat = page_table.reshape(-1)
    ckv_p = _compress_prompt(cmp_p.reshape(BATCH, SEQ, KV_SUB, HEAD_DIM), logits_x, cmp_w1[l], cmp_w2[l], g_kc[l])
    new_cmp = z[NP:, COL_KVC:COL_KVC + KV_ROW].reshape(DEC_BATCH, SROW, KV_SUB, HEAD_DIM)
    ckv_s = _compress_sample(cache_cmp[l].reshape(-1, PAGE_SIZE, KV_SUB, HEAD_DIM), pt_flat, new_cmp, logits_x,
                             cmp_w1[l], cmp_w2[l], g_kc[l])
    ocmp_p, mask_p = _cmp_select(z, ckv_p.reshape(BATCH, -1, KV_ROW), batch=BATCH, rows=SEQ, row0=0, tq=256,
                                 nb=SEQ // SEL_BLOCK, pos_base=0, want_idx=False)
    nb_s = -(-(PAST_LEN + DEC_SEQ) // SEL_BLOCK)
    ocmp_s, _, idx_s = _cmp_select(z, ckv_s.reshape(DEC_BATCH, -1, KV_ROW), batch=DEC_BATCH, rows=SROW, row0=NP,
                                   tq=SROW, nb=nb_s, pos_base=PAST_LEN, want_idx=True)
    osel_p = _flash_prompt(z, "sel", mask_p)
    owin_p = _flash_prompt(z, "win")
    idx = idx_s.reshape(DEC_BATCH, SROW, NSA_KV_HEADS, HEAD_DIM)[:, :DEC_SEQ, :, :N_SEL].astype(jnp.int32)
    blk_no = jnp.concatenate([idx[..., :2], idx[..., 3:]], axis=-1)
    page = jnp.take_along_axis(page_table[:, None, None, :], blk_no // 2, axis=-1)
    phys_half = page * 2 + blk_no % 2
    osel_s = _sel_sample(z, cache_sel[l], phys_half.reshape(-1), blk_no.reshape(-1))
    owin_s = _win_sample(z, cache_win[l].reshape(DEC_BATCH, -1, KV_ROW))

    cat = lambda pr, sa: jnp.concatenate([pr, sa], axis=0)
    y = _mix(cat(ylin_p, ylin_s), cat(ocmp_p, ocmp_s), cat(osel_p, osel_s), cat(owin_p, owin_s), gates,
             g_nsa_out[l], 256)
    h1 = _matmul(y, w_out[l], D_MODEL, 1088, 512, "residual", (x,), ("full",), name="out_proj")

    w_route = jnp.pad(jnp.concatenate([w_group_router[l], w_expert_router[l]], axis=1),
                      ((0, 0), (0, 128 - N_GROUPS - N_EXPERTS)))
    b_route = jnp.pad(jnp.concatenate([b_group_router[l], b_expert_router[l]]),
                      (0, 128 - N_GROUPS - N_EXPERTS)).reshape(1, 128)
    m, route = _router(h1, g_ffn[l], w_route, b_route, 256)
    src, roww, tile_expert, tile_live, pos0, pos1 = _dispatch_plan(route)
    xs = _row_gather(m, src, 256, BF16)
    ys = _experts(xs, roww, tile_expert, tile_live, w_exp_gate[l], w_exp_up[l], w_exp_down[l])
    h2, n2 = _combine(ys, pos0, pos1, h1, g_ple_in[l], 128)

    pp = _matmul(p_rows.astype(BF16), w_ple_proj[l], D_MODEL, 256, D_MODEL, "rowrms",
                 (g_ple_out[l].reshape(1, D_MODEL),), ("row",), name="ple_proj")
    out = _matmul(n2, w_ple_gate[l], D_MODEL, 1088, 512, "ple", (h2, pp), ("full", "full"), name="ple_gate")

    y_prompt = out[:NP].reshape(BATCH, SEQ, D_MODEL)
    y_sample = out[NP:].reshape(DEC_BATCH, SROW, D_MODEL)[:, :DEC_SEQ]
    w_keep = cache_win.shape[2]
    win_prompt = win_p[:, :, SEQ - min(WINDOW, SEQ):]
    win_sample = jnp.concatenate([cache_win[l], win_new[0]], axis=1)[None, :, DEC_SEQ:DEC_SEQ + w_keep]
    return (y_prompt, y_sample, cmp_p, cmp_s, sel_p, sel_s, win_prompt, win_sample,
            st_p[None], st_s[None])
```

```python
import functools
import math

import numpy as np
import jax
import jax.numpy as jnp
from jax import lax
from jax.experimental import pallas as pl
from jax.experimental.pallas import tpu as pltpu

F32 = jnp.float32
BF16 = jnp.bfloat16

D_MODEL = 4096
BATCH = 4
SEQ = 2048
DEC_BATCH = 32
DEC_SEQ = 4
PAST_LEN = 8192
PAGE_SIZE = 128
N_PAGES = PAST_LEN // PAGE_SIZE
LIN_WIDTH = 2048
NSA_WIDTH = 2048
HEAD_DIM = 128
LIN_HEADS = 16
NSA_HEADS = 16
NSA_KV_HEADS = 4
NSA_REP = 4
KV_WIDTH = NSA_KV_HEADS * HEAD_DIM
KV_ROW = 2 * KV_WIDTH
KV_SUB = 2 * NSA_KV_HEADS
CMP_STRIDE = 16
CMP_LEN = 32
CMP_HIDDEN = 256
SEL_BLOCK = 64
N_SEL = 16
WINDOW = 512
N_EXPERTS = 16
N_GROUPS = 4
EXP_PER_GROUP = 4
D_EXPERT = 1024
PLE_DIM = 256
EPS = 1e-6
NEG = -1e30
FORCE = 1e4
SCALE = HEAD_DIM ** -0.5

COL_LQ, COL_LF, COL_LI, COL_LG = 0, 2048, 4096, 6144
COL_NQ = 8192
COL_KVC, COL_KVS, COL_KVW = 10240, 11264, 12288
COL_GATE = 13312
N_MAIN = 13312
N_GATE = 3 * NSA_HEADS

NP = BATCH * SEQ
SROW = 16
NS = DEC_BATCH * SROW
NT = NP + NS

VMEM_LIMIT = 56 * 1024 * 1024

SLOPES = [2.0 ** (-8.0 * (h + 1) / NSA_HEADS) for h in range(NSA_HEADS)]


def _cparams(sem, vmem_limit=VMEM_LIMIT):
    return pltpu.CompilerParams(dimension_semantics=sem, vmem_limit_bytes=vmem_limit)


def _dot(a, b):
    return jnp.dot(a, b, preferred_element_type=F32)


def _dot_nt(a, b):
    return lax.dot_general(a, b, (((1,), (1,)), ((), ())), preferred_element_type=F32)


def _dot_tn(a, b):
    return lax.dot_general(a, b, (((0,), (0,)), ((), ())), preferred_element_type=F32)


def _split2(x):
    hi = x.astype(BF16)
    return hi, (x - hi.astype(F32)).astype(BF16)


def _dot3(a, b, form):
    a_hi, a_lo = _split2(a)
    b_hi, b_lo = _split2(b)
    if form == "tn":
        m = a.shape[1]
        two = _dot_tn(jnp.concatenate([a_hi, a_lo], axis=1), b_hi)
        return two[:m] + two[m:] + _dot_tn(a_hi, b_lo)
    f = _dot_nt if form == "nt" else _dot
    m = a.shape[0]
    two = f(jnp.concatenate([a_hi, a_lo], axis=0), b_hi)
    return two[:m] + two[m:] + f(a_hi, b_lo)


class _Mxu:
    def __init__(self, precise):
        self.precise = precise

    def op(self, x):
        return x.astype(F32) if self.precise else x.astype(BF16)

    def dot(self, a, b):
        return _dot3(a, b, "nn") if self.precise else _dot(a, b)

    def dot_nt(self, a, b):
        return _dot3(a, b, "nt") if self.precise else _dot_nt(a, b)

    def dot_tn(self, a, b):
        return _dot3(a, b, "tn") if self.precise else _dot_tn(a, b)


def _split3(x):
    a = x.astype(BF16)
    r = x - a.astype(F32)
    b = r.astype(BF16)
    c = (r - b.astype(F32)).astype(BF16)
    return a, b, c


def _rms_rows(x, g):
    ms = jnp.mean(x * x, axis=-1, keepdims=True)
    return x * lax.rsqrt(ms + EPS) * g


def _silu(x):
    return x * jax.nn.sigmoid(x)


def _two_group_specs(tm, width):
    npt = NP // tm
    return [pl.BlockSpec((tm, width), lambda i: (jnp.minimum(i, npt - 1), 0)),
            pl.BlockSpec((tm, width), lambda i: (jnp.maximum(i - npt, 0), 0))]


def _pick_group(p_ref, s_ref, tm):
    return jnp.where(pl.program_id(0) < NP // tm, p_ref[...], s_ref[...])


def _rms_cast_kernel(x_ref, g_ref, o_ref):
    o_ref[...] = _rms_rows(x_ref[...], g_ref[...]).astype(o_ref.dtype)


def _rms_cast(x, g, tm, dtype):
    n, d = x.shape
    return pl.pallas_call(
        _rms_cast_kernel,
        grid=(n // tm,),
        in_specs=[pl.BlockSpec((tm, d), lambda i: (i, 0)), pl.BlockSpec((1, d), lambda i: (0, 0))],
        out_specs=pl.BlockSpec((tm, d), lambda i: (i, 0)),
        out_shape=jax.ShapeDtypeStruct((n, d), dtype),
        compiler_params=_cparams(("arbitrary",)),
        name="rms_cast",
    )(x, g.reshape(1, d))


def _mm_kernel(flag_ref, x_ref, w_ref, *rest, epilogue, n_extra, precise):
    extras = rest[:n_extra]
    o_ref = rest[n_extra]
    wb_ref = rest[n_extra + 1]
    j = pl.program_id(0)

    @pl.when(pl.program_id(1) == 0)
    def _():
        w = w_ref[...]
        hi = w.astype(BF16)
        wb_ref[0] = hi
        if precise:
            wb_ref[1] = (w - hi.astype(F32)).astype(BF16)

    if precise:
        x = x_ref[...]
        x_hi = x.astype(BF16)
        x_lo = (x - x_hi.astype(F32)).astype(BF16)
        acc = _dot(x_hi, wb_ref[0]) + (_dot(x_hi, wb_ref[1]) + _dot(x_lo, wb_ref[0]))
    else:
        acc = _dot(x_ref[...], wb_ref[0])
    if epilogue == "groupnorm":
        gain_ref, sel_ref = extras
        normed = flag_ref[j] > 0

        @pl.when(normed)
        def _():
            for c in range(acc.shape[1] // HEAD_DIM):
                cs = slice(c * HEAD_DIM, (c + 1) * HEAD_DIM)
                blk = acc[:, cs]
                o_ref[:, cs] = jnp.where(sel_ref[:, cs] > 0, _rms_rows(blk, gain_ref[:, cs]), blk)

        @pl.when(jnp.logical_not(normed))
        def _():
            o_ref[...] = acc
    elif epilogue == "sigmoid":
        o_ref[...] = jax.nn.sigmoid(acc)
    elif epilogue == "none":
        o_ref[...] = acc
    elif epilogue == "rowrms":
        o_ref[...] = _rms_rows(acc, extras[0][...])
    elif epilogue == "ple":
        h_ref, pp_ref = extras
        o_ref[...] = h_ref[...] + jax.nn.sigmoid(acc) * pp_ref[...]
    else:
        raise ValueError(epilogue)


def _matmul(x, w, n_out, tm, tn, epilogue, extras=(), extra_kinds=(), flags=None, name="mm", row0=0, m=None):
    precise = x.dtype == F32
    k = x.shape[1]
    m = x.shape[0] if m is None else m
    blk0 = row0 // tm
    n_tiles = n_out // tn
    if flags is None:
        flags = jnp.zeros((n_tiles,), jnp.int32)
    specs = [pl.BlockSpec((tm, k), lambda j, i, f: (blk0 + i, 0)), pl.BlockSpec((k, tn), lambda j, i, f: (0, j))]
    for kind in extra_kinds:
        if kind == "row":
            specs.append(pl.BlockSpec((1, tn), lambda j, i, f: (0, j)))
        else:
            specs.append(pl.BlockSpec((tm, tn), lambda j, i, f: (blk0 + i, j)))
    return pl.pallas_call(
        functools.partial(_mm_kernel, epilogue=epilogue, n_extra=len(extras), precise=precise),
        grid_spec=pltpu.PrefetchScalarGridSpec(
            num_scalar_prefetch=1,
            grid=(n_tiles, m // tm),
            in_specs=specs,
            out_specs=pl.BlockSpec((tm, tn), lambda j, i, f: (i, j)),
            scratch_shapes=[pltpu.VMEM((2 if precise else 1, k, tn), BF16)],
        ),
        out_shape=jax.ShapeDtypeStruct((m, n_out), F32),
        compiler_params=_cparams(("arbitrary", "arbitrary")),
        name=name,
    )(flags, x, w, *extras)


def _hgrn_masks(c):
    levels = int(math.log2(c))
    sums = np.zeros((levels + 2, c, c), np.float32)
    pair = np.zeros((levels + 1, c, c), np.float32)
    pair[0] = np.eye(c)
    for lv in range(levels):
        h = 1 << lv
        for t in range(c):
            base = (t // (2 * h)) * 2 * h
            m = base + h - 1
            if t % (2 * h) >= h:
                sums[lv, t, m + 1:t + 1] = 1.0
                pair[lv + 1, t, base:base + h] = 1.0
            else:
                sums[lv, t, t + 1:m + 1] = 1.0
    for t in range(c):
        sums[levels, t, :t + 1] = 1.0
        sums[levels + 1, t, t + 1:] = 1.0
    return sums.reshape((levels + 2) * c, c), pair


def _hgrn_kernel(q_ref, f_ref, v_ref, g_ref, lb_ref, gout_ref, sums_ref, pair_ref, *rest, c, n_chunks, t_valid,
                 hp, has_s0, precise):
    mx = _Mxu(precise)
    y_ref, s_ref, st_scr = rest[-3:]
    for hh in range(hp):
        st_scr[hh] = rest[0][hh].T if has_s0 else jnp.zeros((HEAD_DIM, HEAD_DIM), F32)
    levels = int(math.log2(c))
    row = lax.broadcasted_iota(jnp.int32, (c, 1), 0)

    def chunk(ci, carry):
        r0 = pl.multiple_of(ci * c, c)
        for hh in range(hp):
            hs = slice(hh * HEAD_DIM, (hh + 1) * HEAD_DIM)
            lb = lb_ref[hh]
            q = q_ref[pl.ds(r0, c), hs]
            z = f_ref[pl.ds(r0, c), hs]
            v = v_ref[pl.ds(r0, c), hs]
            logf = jnp.log(lb + (1.0 - lb) * jax.nn.sigmoid(z))
            k = (1.0 - lb) * jax.nn.sigmoid(-z)
            if t_valid < c:
                live = row < t_valid
                logf = jnp.where(live, logf, 0.0)
                k = jnp.where(live, k, 0.0)
            parts = jnp.concatenate(_split3(logf), axis=1)
            r = _dot(sums_ref[...], parts)
            ex = jnp.exp(r[:, :HEAD_DIM] + r[:, HEAD_DIM:2 * HEAD_DIM] + r[:, 2 * HEAD_DIM:])
            attn = pair_ref[0] * mx.dot_nt(mx.op(q), mx.op(k))
            for lv in range(levels):
                w = ex[lv * c:(lv + 1) * c]
                attn = attn + pair_ref[lv + 1] * mx.dot_nt(mx.op(q * w), mx.op(k * w))
            e_cum = ex[levels * c:(levels + 1) * c]
            e_rev = ex[(levels + 1) * c:(levels + 2) * c]
            st = st_scr[hh]
            vb = mx.op(v)
            o = mx.dot_nt(mx.op(q * e_cum), mx.op(st)) + mx.dot(mx.op(attn), vb)
            st_scr[hh] = st * e_cum[c - 1:c, :] + mx.dot_tn(vb, mx.op(k * e_rev))
            y_ref[pl.ds(r0, c), hs] = _rms_rows(o, gout_ref[...]) * _silu(g_ref[pl.ds(r0, c), hs])
        return carry

    lax.fori_loop(0, n_chunks, chunk, 0)
    for hh in range(hp):
        s_ref[hh] = st_scr[hh].T


def _hgrn(z, lb, g_lin_out, s0, *, batch, rows, c, t_valid, hp, precise):
    sums, pair = _hgrn_masks(c)
    has_s0 = s0 is not None
    w = hp * HEAD_DIM

    def zspec(col):
        cb = col // w
        return pl.BlockSpec((rows, w), lambda b, h: (b, cb + h))

    const2 = lambda b, h: (0, 0)
    st_spec = pl.BlockSpec((None, hp, HEAD_DIM, HEAD_DIM), lambda b, h: (b, h, 0, 0))
    in_specs = [zspec(COL_LQ), zspec(COL_LF), zspec(COL_LI), zspec(COL_LG),
                pl.BlockSpec((hp, 1, HEAD_DIM), lambda b, h: (h, 0, 0)),
                pl.BlockSpec((1, HEAD_DIM), const2),
                pl.BlockSpec(sums.shape, const2),
                pl.BlockSpec(pair.shape, lambda b, h: (0, 0, 0))]
    args = [z, z, z, z, lb.reshape(LIN_HEADS, 1, HEAD_DIM), g_lin_out.reshape(1, HEAD_DIM),
            jnp.asarray(sums, BF16), jnp.asarray(pair, F32)]
    if has_s0:
        in_specs.append(st_spec)
        args.append(s0)
    return pl.pallas_call(
        functools.partial(_hgrn_kernel, c=c, n_chunks=rows // c, t_valid=t_valid, hp=hp, has_s0=has_s0,
                          precise=precise),
        grid=(batch, LIN_HEADS // hp),
        in_specs=in_specs,
        out_specs=[pl.BlockSpec((rows, w), lambda b, h: (b, h)), st_spec],
        out_shape=[jax.ShapeDtypeStruct((batch * rows, LIN_WIDTH), F32),
                   jax.ShapeDtypeStruct((batch, LIN_HEADS, HEAD_DIM, HEAD_DIM), F32)],
        scratch_shapes=[pltpu.VMEM((hp, HEAD_DIM, HEAD_DIM), F32)],
        compiler_params=_cparams(("arbitrary", "arbitrary")),
        name="hgrn2",
    )(*args)


def _pool_weights(logit_ref):
    lg = logit_ref[...]
    e = jnp.exp(lg - jnp.max(lg, axis=0, keepdims=True))
    a = e / jnp.sum(e, axis=0, keepdims=True)
    return a[:CMP_STRIDE], a[CMP_STRIDE:]


def _pool(rows, a1, a2):
    n = rows.shape[0] // CMP_STRIDE
    r = rows.reshape(n, CMP_STRIDE, KV_SUB, HEAD_DIM)
    return jnp.sum(r * a1[None], axis=1), jnp.sum(r * a2[None], axis=1)


def _cmp_mlp(pooled, w1_ref, w2_ref, gkc_ref, precise):
    mx = _Mxu(precise)
    r = pooled.shape[0]
    xs = pooled.reshape(r * KV_SUB, HEAD_DIM)
    xb = mx.op(xs)
    outs = []
    for t in range(2):
        hid = _silu(mx.dot(xb, mx.op(w1_ref[t])))
        outs.append(xs + mx.dot(mx.op(hid), mx.op(w2_ref[t])))
    is_key = (lax.broadcasted_iota(jnp.int32, (r * KV_SUB, 1), 0) & 1) == 0
    out = jnp.where(is_key, _rms_rows(outs[0], gkc_ref[...]), outs[1])
    return out.reshape(r, KV_SUB, HEAD_DIM)


def _compress_prompt_kernel(kv_ref, logit_ref, w1_ref, w2_ref, gkc_ref, o_ref):
    a1, a2 = _pool_weights(logit_ref)
    p1, p2 = _pool(kv_ref[...], a1, a2)
    pooled = p1 + jnp.concatenate([p2[1:], p2[:1]], axis=0)
    o_ref[...] = _cmp_mlp(pooled, w1_ref, w2_ref, gkc_ref, precise=False)


def _compress_prompt(kv, logits_x, w1, w2, g_kc):
    nblk = SEQ // CMP_STRIDE
    cst = lambda b: (0, 0)
    cst3 = lambda b: (0, 0, 0)
    return pl.pallas_call(
        _compress_prompt_kernel,
        grid=(BATCH,),
        in_specs=[pl.BlockSpec((None, SEQ, KV_SUB, HEAD_DIM), lambda b: (b, 0, 0, 0)),
                  pl.BlockSpec((CMP_LEN, KV_SUB, HEAD_DIM), cst3),
                  pl.BlockSpec(w1.shape, cst3), pl.BlockSpec(w2.shape, cst3),
                  pl.BlockSpec((1, HEAD_DIM), cst)],
        out_specs=pl.BlockSpec((None, nblk, KV_SUB, HEAD_DIM), lambda b: (b, 0, 0, 0)),
        out_shape=jax.ShapeDtypeStruct((BATCH, nblk, KV_SUB, HEAD_DIM), F32),
        compiler_params=_cparams(("arbitrary",)),
        name="compress_prompt",
    )(kv, logits_x, w1, w2, g_kc.reshape(1, HEAD_DIM))


PAGES_PER_STEP = 8


def _compress_sample_kernel(pt_ref, *refs):
    pages = refs[:PAGES_PER_STEP]
    nxt_ref, new_ref, logit_ref, w1_ref, w2_ref, gkc_ref, o_ref = refs[PAGES_PER_STEP:]
    a1, a2 = _pool_weights(logit_ref)
    last = pl.program_id(1) == pl.num_programs(1) - 1
    nxt = jnp.where(last, new_ref[...], nxt_ref[...])
    tail = jnp.sum(nxt * a2, axis=0, keepdims=True)
    pooled = [None] * PAGES_PER_STEP
    for kk in reversed(range(PAGES_PER_STEP)):
        p1, p2 = _pool(pages[kk][...], a1, a2)
        pooled[kk] = p1 + jnp.concatenate([p2[1:], tail], axis=0)
        tail = p2[:1]
    o_ref[...] = _cmp_mlp(jnp.concatenate(pooled, axis=0), w1_ref, w2_ref, gkc_ref, precise=True)


def _compress_sample(cache, pt_flat, new_rows, logits_x, w1, w2, g_kc):
    steps = N_PAGES // PAGES_PER_STEP
    rows = PAGES_PER_STEP * PAGE_SIZE // CMP_STRIDE
    cst = lambda b, s, pt: (0, 0)
    cst3 = lambda b, s, pt: (0, 0, 0)

    def page_spec(kk):
        return pl.BlockSpec((None, PAGE_SIZE, KV_SUB, HEAD_DIM),
                            lambda b, s, pt: (pt[b * N_PAGES + s * PAGES_PER_STEP + kk], 0, 0, 0))

    nxt_spec = pl.BlockSpec(
        (None, CMP_STRIDE, KV_SUB, HEAD_DIM),
        lambda b, s, pt: (pt[b * N_PAGES + jnp.minimum((s + 1) * PAGES_PER_STEP, N_PAGES - 1)], 0, 0, 0))
    new_spec = pl.BlockSpec((None, CMP_STRIDE, KV_SUB, HEAD_DIM), lambda b, s, pt: (b, 0, 0, 0))
    nc = PAST_LEN // CMP_STRIDE
    return pl.pallas_call(
        _compress_sample_kernel,
        grid_spec=pltpu.PrefetchScalarGridSpec(
            num_scalar_prefetch=1,
            grid=(DEC_BATCH, steps),
            in_specs=[page_spec(kk) for kk in range(PAGES_PER_STEP)] + [
                nxt_spec, new_spec, pl.BlockSpec((CMP_LEN, KV_SUB, HEAD_DIM), cst3),
                pl.BlockSpec(w1.shape, cst3), pl.BlockSpec(w2.shape, cst3), pl.BlockSpec((1, HEAD_DIM), cst)],
            out_specs=pl.BlockSpec((None, rows, KV_SUB, HEAD_DIM), lambda b, s, pt: (b, s, 0, 0)),
        ),
        out_shape=jax.ShapeDtypeStruct((DEC_BATCH, nc, KV_SUB, HEAD_DIM), F32),
        compiler_params=_cparams(("arbitrary", "arbitrary")),
        name="compress_sample",
    )(pt_flat, *([cache] * (PAGES_PER_STEP + 1)), new_rows, logits_x, w1, w2, g_kc.reshape(1, HEAD_DIM))


def _overlap_matrix(ncp, nbp, nb):
    c0 = np.arange(ncp)[:, None] * CMP_STRIDE
    b0 = np.arange(nbp)[None, :] * SEL_BLOCK
    ov = (c0 < b0 + SEL_BLOCK) & (c0 + CMP_LEN > b0) & (np.arange(nbp)[None, :] < nb)
    return ov.astype(np.float32)


def _cmp_select_kernel(q_ref, ckv_ref, ov_ref, o_ref, m_ref, *maybe_idx, tq, nb, pos_base, tiled, precise):
    mx = _Mxu(precise)
    ncp = ckv_ref.shape[0]
    nbp = ov_ref.shape[1]
    t0 = pos_base + (pl.program_id(1) * tq if tiled else 0)
    qp = t0 + lax.broadcasted_iota(jnp.int32, (tq, 1), 0)
    cend = lax.broadcasted_iota(jnp.int32, (1, ncp), 1) * CMP_STRIDE + (CMP_LEN - 1)
    valid = qp >= cend
    validf = valid.astype(F32)
    dist = (qp - cend).astype(F32)
    n_i = lax.broadcasted_iota(jnp.int32, (1, nbp), 1)
    qblk = lax.shift_right_arithmetic(qp, int(math.log2(SEL_BLOCK)))
    forced = (n_i == 0) | (n_i == qblk) | (n_i == qblk - 1)
    future = n_i * SEL_BLOCK > qp
    lane = lax.broadcasted_iota(jnp.int32, (1, HEAD_DIM), 1)
    for g in range(NSA_KV_HEADS):
        gs = slice(g * HEAD_DIM, (g + 1) * HEAD_DIM)
        kcg = mx.op(ckv_ref[:, 2 * g * HEAD_DIM:(2 * g + 1) * HEAD_DIM])
        vcg = mx.op(ckv_ref[:, (2 * g + 1) * HEAD_DIM:(2 * g + 2) * HEAD_DIM])
        q4 = mx.op(jnp.concatenate([q_ref[:, (g * NSA_REP + r) * HEAD_DIM:(g * NSA_REP + r + 1) * HEAD_DIM]
                                    for r in range(NSA_REP)], axis=0))
        s4 = mx.dot_nt(q4, kcg) * SCALE
        psum = jnp.zeros((tq, ncp), F32)
        for r in range(NSA_REP):
            h = g * NSA_REP + r
            s = jnp.where(valid, s4[r * tq:(r + 1) * tq] - SLOPES[h] * dist, NEG)
            e = jnp.exp(s - jnp.max(s, axis=-1, keepdims=True))
            p = e / jnp.sum(e, axis=-1, keepdims=True) * validf
            o_ref[:, h * HEAD_DIM:(h + 1) * HEAD_DIM] = mx.dot(mx.op(p), vcg)
            psum = psum + p
        ov = ov_ref[...]
        imp = sum(_dot(part, ov) for part in _split3(psum))
        score = jnp.where(forced, FORCE, jnp.where(future, -1.0, imp))
        score = jnp.where(n_i < nb, score, -2.0)
        rank = jnp.zeros((tq, nbp), F32)
        for m in range(nb):
            col = score[:, m:m + 1]
            ahead = (col > score) | ((col == score) & (n_i > m))
            rank = rank + jnp.where(ahead, 1.0, 0.0)
        chosen = (rank < float(N_SEL)) & (n_i < nb) & jnp.logical_not(future)
        m_ref[:, g * nbp:(g + 1) * nbp] = chosen.astype(F32)
        if maybe_idx:
            nf = n_i.astype(F32)
            idx = jnp.zeros((tq, HEAD_DIM), F32)
            for jj in range(N_SEL):
                pick = jnp.sum(jnp.where(rank == float(jj), nf, 0.0), axis=-1, keepdims=True)
                idx = jnp.where(lane == jj, pick, idx)
            maybe_idx[0][:, gs] = idx


def _cmp_select(z, ckv, *, batch, rows, tq, nb, pos_base, want_idx, precise):
    ncp = ckv.shape[1]
    nbp = -(-nb // 128) * 128
    ov = jnp.asarray(_overlap_matrix(ncp, nbp, nb), BF16)
    n_q = rows // tq
    out_specs = [pl.BlockSpec((tq, NSA_WIDTH), lambda b, i: (b * n_q + i, 0)),
                 pl.BlockSpec((tq, NSA_KV_HEADS * nbp), lambda b, i: (b * n_q + i, 0))]
    out_shape = [jax.ShapeDtypeStruct((batch * rows, NSA_WIDTH), F32),
                 jax.ShapeDtypeStruct((batch * rows, NSA_KV_HEADS * nbp), F32)]
    if want_idx:
        out_specs.append(pl.BlockSpec((tq, NSA_KV_HEADS * HEAD_DIM), lambda b, i: (b * n_q + i, 0)))
        out_shape.append(jax.ShapeDtypeStruct((batch * rows, NSA_KV_HEADS * HEAD_DIM), F32))
    return pl.pallas_call(
        functools.partial(_cmp_select_kernel, tq=tq, nb=nb, pos_base=pos_base, tiled=n_q > 1, precise=precise),
        grid=(batch, n_q),
        in_specs=[pl.BlockSpec((tq, NSA_WIDTH), lambda b, i: (b * n_q + i, COL_NQ // NSA_WIDTH)),
                  pl.BlockSpec((None, ncp, KV_ROW), lambda b, i: (b, 0, 0)),
                  pl.BlockSpec(ov.shape, lambda b, i: (0, 0))],
        out_specs=out_specs,
        out_shape=out_shape,
        compiler_params=_cparams(("arbitrary", "arbitrary")),
        name="cmp_select",
    )(z, ckv, ov)


FLASH_TQ = 256
FLASH_TK = 256


def _flash_kernel(slope_ref, q_ref, kv_ref, *rest, mode):
    if mode == "sel":
        mask_ref, exp_ref, o_ref, m_scr, l_scr, acc_scr = rest
    else:
        o_ref, m_scr, l_scr, acc_scr = rest
    tq, tk = FLASH_TQ, FLASH_TK
    g = pl.program_id(1)
    qi = pl.program_id(2)
    kj = pl.program_id(3)
    if mode == "sel":
        ka = kj
        needed = kj * tk <= qi * tq + (tq - 1)
    else:
        ka = qi * (tq // tk) - WINDOW // tk + kj
        needed = ka >= 0

    @pl.when(kj == 0)
    def _():
        m_scr[...] = jnp.full_like(m_scr, NEG)
        l_scr[...] = jnp.zeros_like(l_scr)
        acc_scr[...] = jnp.zeros_like(acc_scr)

    @pl.when(needed)
    def _():
        q4 = jnp.concatenate([q_ref[:, r * HEAD_DIM:(r + 1) * HEAD_DIM] for r in range(NSA_REP)],
                             axis=0).astype(BF16)
        kb = kv_ref[:, :HEAD_DIM].astype(BF16)
        vb = kv_ref[:, HEAD_DIM:].astype(BF16)
        s4 = _dot_nt(q4, kb) * SCALE
        qpos = qi * tq + lax.broadcasted_iota(jnp.int32, (tq, 1), 0)
        kpos = ka * tk + lax.broadcasted_iota(jnp.int32, (1, tk), 1)
        d = qpos - kpos
        ok = d >= 0
        if mode == "sel":
            ok = ok & (_dot(mask_ref[...].astype(BF16), exp_ref[...]) > 0.5)
        else:
            ok = ok & (d < WINDOW)
        df = d.astype(F32)
        for r in range(NSA_REP):
            s = jnp.where(ok, s4[r * tq:(r + 1) * tq] - slope_ref[g * NSA_REP + r] * df, NEG)
            m_prev = m_scr[r]
            m_new = jnp.maximum(m_prev, jnp.max(s, axis=-1, keepdims=True))
            p = jnp.where(ok, jnp.exp(s - m_new), 0.0)
            alpha = jnp.exp(m_prev - m_new)
            l_scr[r] = alpha * l_scr[r] + jnp.sum(p, axis=-1, keepdims=True)
            acc_scr[r] = alpha * acc_scr[r] + _dot(p.astype(BF16), vb)
            m_scr[r] = m_new

    @pl.when(kj == pl.num_programs(3) - 1)
    def _():
        for r in range(NSA_REP):
            o_ref[:, r * HEAD_DIM:(r + 1) * HEAD_DIM] = acc_scr[r] / l_scr[r]


def _flash_prompt(z, mode, mask=None):
    tq, tk = FLASH_TQ, FLASH_TK
    n_q = SEQ // tq
    slopes = jnp.asarray(SLOPES, F32)
    qw = NSA_REP * HEAD_DIM
    kvw = 2 * HEAD_DIM
    if mode == "sel":
        n_kv = SEQ // tk
        col0 = COL_KVS // kvw
        kv_map = lambda b, g, i, j: (b * n_kv + jnp.minimum(j, (i * tq + tq - 1) // tk), col0 + g)
    else:
        n_kv = WINDOW // tk + tq // tk
        col0 = COL_KVW // kvw
        kv_map = lambda b, g, i, j: (b * (SEQ // tk) + jnp.maximum(i * (tq // tk) - WINDOW // tk + j, 0), col0 + g)
    in_specs = [pl.BlockSpec(memory_space=pltpu.SMEM),
                pl.BlockSpec((tq, qw), lambda b, g, i, j: (b * n_q + i, COL_NQ // qw + g)),
                pl.BlockSpec((tk, kvw), kv_map)]
    args = [slopes, z, z]
    if mode == "sel":
        nbp = mask.shape[1] // NSA_KV_HEADS
        blk = (np.arange(SEQ)[None, :] // SEL_BLOCK) == np.arange(nbp)[:, None]
        expand = jnp.asarray(blk.reshape(nbp, n_kv, tk).transpose(1, 0, 2).astype(np.float32), BF16)
        in_specs += [pl.BlockSpec((tq, nbp), lambda b, g, i, j: (b * n_q + i, g)),
                     pl.BlockSpec((None, nbp, tk), lambda b, g, i, j: (jnp.minimum(j, (i * tq + tq - 1) // tk), 0, 0))]
        args += [mask, expand]
    return pl.pallas_call(
        functools.partial(_flash_kernel, mode=mode),
        grid=(BATCH, NSA_KV_HEADS, n_q, n_kv),
        in_specs=in_specs,
        out_specs=pl.BlockSpec((tq, qw), lambda b, g, i, j: (b * n_q + i, g)),
        out_shape=jax.ShapeDtypeStruct((NP, NSA_WIDTH), F32),
        scratch_shapes=[pltpu.VMEM((NSA_REP, tq, 1), F32), pltpu.VMEM((NSA_REP, tq, 1), F32),
                        pltpu.VMEM((NSA_REP, tq, HEAD_DIM), F32)],
        compiler_params=_cparams(("arbitrary",) * 4),
        name="flash_" + mode,
    )(*args)


def _softmax_parts(scores, oks):
    mx = functools.reduce(jnp.maximum, [jnp.max(jnp.where(ok, s, NEG), axis=-1, keepdims=True)
                                        for s, ok in zip(scores, oks)])
    es = [jnp.where(ok, jnp.exp(jnp.where(ok, s, NEG) - mx), 0.0) for s, ok in zip(scores, oks)]
    den = functools.reduce(lambda a, b: a + b, [jnp.sum(e, axis=-1, keepdims=True) for e in es])
    return [e / den for e in es]


def _win_sample_kernel(q_ref, cache_ref, new_ref, o_ref):
    mx = _Mxu(True)
    wk = cache_ref.shape[0] // KV_SUB
    row = lax.broadcasted_iota(jnp.int32, (NSA_REP * SROW, 1), 0)
    qi = row & (SROW - 1)
    jc = lax.broadcasted_iota(jnp.int32, (1, wk), 1)
    jn = lax.broadcasted_iota(jnp.int32, (1, SROW), 1)
    d_c = qi + (wk - jc)
    d_n = qi - jn
    ok_c = d_c < WINDOW
    ok_n = (d_n >= 0) & (jn < DEC_SEQ)
    for g in range(NSA_KV_HEADS):
        kcol = slice(g * 2 * HEAD_DIM, g * 2 * HEAD_DIM + HEAD_DIM)
        vcol = slice(g * 2 * HEAD_DIM + HEAD_DIM, (g + 1) * 2 * HEAD_DIM)
        k_c = cache_ref[pl.ds(2 * g, wk, stride=KV_SUB), :]
        v_c = cache_ref[pl.ds(2 * g + 1, wk, stride=KV_SUB), :]
        q4 = jnp.concatenate([q_ref[:, (g * NSA_REP + r) * HEAD_DIM:(g * NSA_REP + r + 1) * HEAD_DIM]
                              for r in range(NSA_REP)], axis=0)
        s_c = mx.dot_nt(q4, k_c) * SCALE
        s_n = mx.dot_nt(q4, new_ref[:, kcol]) * SCALE
        slope = jnp.concatenate([jnp.full((SROW, 1), SLOPES[g * NSA_REP + r], F32) for r in range(NSA_REP)], axis=0)
        s_c = s_c - slope * d_c.astype(F32)
        s_n = s_n - slope * d_n.astype(F32)
        p_c, p_n = _softmax_parts([s_c, s_n], [ok_c, ok_n])
        o = mx.dot(p_c, v_c) + mx.dot(p_n, new_ref[:, vcol])
        for r in range(NSA_REP):
            h = g * NSA_REP + r
            o_ref[:, h * HEAD_DIM:(h + 1) * HEAD_DIM] = o[r * SROW:(r + 1) * SROW]


def _win_sample(z, cache_win):
    wk8 = cache_win.shape[1]
    return pl.pallas_call(
        _win_sample_kernel,
        grid=(DEC_BATCH,),
        in_specs=[pl.BlockSpec((SROW, NSA_WIDTH), lambda b: (b, COL_NQ // NSA_WIDTH)),
                  pl.BlockSpec((None, wk8, HEAD_DIM), lambda b: (b, 0, 0)),
                  pl.BlockSpec((SROW, KV_ROW), lambda b: (b, COL_KVW // KV_ROW))],
        out_specs=pl.BlockSpec((SROW, NSA_WIDTH), lambda b: (b, 0)),
        out_shape=jax.ShapeDtypeStruct((NS, NSA_WIDTH), F32),
        compiler_params=_cparams(("arbitrary",)),
        name="win_sample",
    )(z, cache_win, z)


N_GATHER = N_SEL - 1


def _sel_sample_kernel(phys_ref, blk_ref, slope_ref, q_ref, new_ref, *rest):
    blocks = rest[:N_GATHER]
    o_ref = rest[-1]
    b = pl.program_id(0)
    g = pl.program_id(1)
    i = pl.program_id(2)

    @pl.when(i == 0)
    def _():
        o_ref[...] = jnp.zeros_like(o_ref)

    base = ((b * DEC_SEQ + i) * NSA_KV_HEADS + g) * N_GATHER
    rr = lax.broadcasted_iota(jnp.int32, (8, 1), 0)
    mine = rr == i
    mx = _Mxu(True)
    nq = 16
    q4 = jnp.concatenate([jnp.sum(jnp.where(mine, q_ref[0:8, r * HEAD_DIM:(r + 1) * HEAD_DIM], 0.0),
                                  axis=0, keepdims=True) for r in range(NSA_REP)]
                         + [jnp.zeros((nq - NSA_REP, HEAD_DIM), F32)], axis=0)
    rq = lax.broadcasted_iota(jnp.int32, (nq, 1), 0)
    slope = jnp.zeros((nq, 1), F32)
    for r in range(NSA_REP):
        slope = jnp.where(rq == r, slope_ref[g * NSA_REP + r], slope)
    qp = PAST_LEN + i
    lane2 = lax.broadcasted_iota(jnp.int32, (1, 2 * SEL_BLOCK), 1)
    off = lax.shift_right_logical(lane2, 1)
    rows = jnp.concatenate([blocks[jj][...].reshape(2 * SEL_BLOCK, HEAD_DIM) for jj in range(N_GATHER)], axis=0)
    d = jnp.concatenate([qp - (blk_ref[base + jj] * SEL_BLOCK + off) for jj in range(N_GATHER)], axis=1)
    is_key = jnp.concatenate([(lane2 & 1) == 0] * N_GATHER, axis=1)
    jn = lax.broadcasted_iota(jnp.int32, (1, SROW), 1)
    d_n = i - jn
    scores = [mx.dot_nt(q4, rows) * SCALE - slope * d.astype(F32),
              mx.dot_nt(q4, new_ref[:, :HEAD_DIM]) * SCALE - slope * d_n.astype(F32)]
    oks = [jnp.broadcast_to((d >= 0) & is_key, scores[0].shape),
           jnp.broadcast_to((d_n >= 0) & (jn < DEC_SEQ), scores[1].shape)]
    p_old, p_new = _softmax_parts(scores, oks)
    o = mx.dot(pltpu.roll(p_old, 1, axis=1), rows) + mx.dot(p_new, new_ref[:, HEAD_DIM:])
    for r in range(NSA_REP):
        cs = slice(r * HEAD_DIM, (r + 1) * HEAD_DIM)
        o_ref[0:8, cs] = jnp.where(mine, o[r:r + 1], o_ref[0:8, cs])


def _sel_sample(z, cache_sel, phys_half, blk_no):
    kvw = 2 * HEAD_DIM
    qw = NSA_REP * HEAD_DIM
    halves = cache_sel.reshape(-1, SEL_BLOCK, NSA_KV_HEADS, 2, HEAD_DIM)

    def blk_spec(jj):
        return pl.BlockSpec(
            (None, SEL_BLOCK, None, 2, HEAD_DIM),
            lambda b, g, i, ph, bn: (ph[((b * DEC_SEQ + i) * NSA_KV_HEADS + g) * N_GATHER + jj], 0, g, 0, 0))

    return pl.pallas_call(
        _sel_sample_kernel,
        grid_spec=pltpu.PrefetchScalarGridSpec(
            num_scalar_prefetch=2,
            grid=(DEC_BATCH, NSA_KV_HEADS, DEC_SEQ),
            in_specs=[pl.BlockSpec(memory_space=pltpu.SMEM),
                      pl.BlockSpec((SROW, qw), lambda b, g, i, ph, bn: (b, COL_NQ // qw + g)),
                      pl.BlockSpec((SROW, kvw), lambda b, g, i, ph, bn: (b, COL_KVS // kvw + g))]
            + [blk_spec(jj) for jj in range(N_GATHER)],
            out_specs=pl.BlockSpec((SROW, qw), lambda b, g, i, ph, bn: (b, g)),
        ),
        out_shape=jax.ShapeDtypeStruct((NS, NSA_WIDTH), F32),
        compiler_params=_cparams(("arbitrary",) * 3),
        name="sel_sample",
    )(phys_half, blk_no, jnp.asarray(SLOPES, F32), z, z, *([halves] * N_GATHER))


def _mix_kernel(ylin_ref, cmp_ref, sel_ref, win_ref, gate_ref, gn_ref, y_ref):
    gates = gate_ref[...]
    parts = []
    for h in range(NSA_HEADS):
        hs = slice(h * HEAD_DIM, (h + 1) * HEAD_DIM)
        parts.append(gates[:, h:h + 1] * cmp_ref[:, hs]
                     + gates[:, NSA_HEADS + h:NSA_HEADS + h + 1] * sel_ref[:, hs]
                     + gates[:, 2 * NSA_HEADS + h:2 * NSA_HEADS + h + 1] * win_ref[:, hs])
    o = jnp.concatenate(parts, axis=1)
    y_ref[:, :LIN_WIDTH] = ylin_ref[...].astype(y_ref.dtype)
    y_ref[:, LIN_WIDTH:] = _rms_rows(o, gn_ref[...]).astype(y_ref.dtype)


def _mix(ylin, o_cmp, o_sel, o_win, gates, g_nsa_out, tm, dtype):
    n = ylin.shape[0]
    row = lambda i: (i, 0)
    return pl.pallas_call(
        _mix_kernel,
        grid=(n // tm,),
        in_specs=[pl.BlockSpec((tm, LIN_WIDTH), row)] + [pl.BlockSpec((tm, NSA_WIDTH), row)] * 3
        + [pl.BlockSpec((tm, 128), row), pl.BlockSpec((1, NSA_WIDTH), lambda i: (0, 0))],
        out_specs=pl.BlockSpec((tm, D_MODEL), row),
        out_shape=jax.ShapeDtypeStruct((n, D_MODEL), dtype),
        compiler_params=_cparams(("arbitrary",)),
        name="mix",
    )(ylin, o_cmp, o_sel, o_win, gates, g_nsa_out.reshape(1, NSA_WIDTH))


def _router_kernel(xp_ref, xs_ref, ap_ref, as_ref, g_ref, wr_ref, br_ref, h_ref, m_ref, r_ref, *, tm):
    h = _pick_group(xp_ref, xs_ref, tm) + _pick_group(ap_ref, as_ref, tm)
    h_ref[...] = h
    m = _rms_rows(h, g_ref[...])
    m_ref[...] = m
    logits = _dot3(m, wr_ref[...], "nn") + br_ref[...]
    lane = lax.broadcasted_iota(jnp.int32, (1, 128), 1)
    big = 1024
    is_g = lane < N_GROUPS
    gl = jnp.where(is_g, logits, -jnp.inf)
    gmax = jnp.max(gl, axis=-1, keepdims=True)
    gsel = jnp.min(jnp.where(gl == gmax, lane, big), axis=-1, keepdims=True)
    gw = 1.0 / jnp.sum(jnp.where(is_g, jnp.exp(gl - gmax), 0.0), axis=-1, keepdims=True)
    e_id = lane - N_GROUPS
    in_grp = (e_id >= 0) & (e_id < N_EXPERTS) & (lax.shift_right_arithmetic(e_id, int(math.log2(EXP_PER_GROUP))) == gsel)
    el = jnp.where(in_grp, logits, -jnp.inf)
    v1 = jnp.max(el, axis=-1, keepdims=True)
    i1 = jnp.min(jnp.where(el == v1, lane, big), axis=-1, keepdims=True)
    el2 = jnp.where(lane == i1, -jnp.inf, el)
    v2 = jnp.max(el2, axis=-1, keepdims=True)
    i2 = jnp.min(jnp.where(el2 == v2, lane, big), axis=-1, keepdims=True)
    t = jnp.exp(v2 - v1)
    w1 = gw / (1.0 + t)
    w2 = gw * t / (1.0 + t)
    out = jnp.where(lane == 0, (i1 - N_GROUPS).astype(F32), 0.0)
    out = jnp.where(lane == 1, (i2 - N_GROUPS).astype(F32), out)
    out = jnp.where(lane == 2, w1, out)
    out = jnp.where(lane == 3, w2, out)
    r_ref[...] = out


def _router(x_p, x_s, attn_p, attn_s, g_ffn, w_route, b_route, tm):
    row = lambda i: (i, 0)
    cst = lambda i: (0, 0)
    return pl.pallas_call(
        functools.partial(_router_kernel, tm=tm),
        grid=(NT // tm,),
        in_specs=_two_group_specs(tm, D_MODEL) + _two_group_specs(tm, D_MODEL) + [
                  pl.BlockSpec((1, D_MODEL), cst),
                  pl.BlockSpec((D_MODEL, 128), cst), pl.BlockSpec((1, 128), cst)],
        out_specs=[pl.BlockSpec((tm, D_MODEL), row), pl.BlockSpec((tm, D_MODEL), row),
                   pl.BlockSpec((tm, 128), row)],
        out_shape=[jax.ShapeDtypeStruct((NT, D_MODEL), F32), jax.ShapeDtypeStruct((NT, D_MODEL), F32),
                   jax.ShapeDtypeStruct((NT, 128), F32)],
        compiler_params=_cparams(("arbitrary",)),
        name="moe_router",
    )(x_p, x_s, attn_p, attn_s, g_ffn.reshape(1, D_MODEL), w_route, b_route)


def _row_gather_kernel(idx_ref, cnt_ref, src_ref, o_ref, buf, sem, *, rows):
    step = pl.program_id(0)
    base = step * rows
    n_real = cnt_ref[step]

    @pl.when(n_real < rows)
    def _():
        buf[...] = jnp.zeros_like(buf)

    def copy(r):
        return pltpu.make_async_copy(src_ref.at[pl.ds(idx_ref[base + r], 1), :], buf.at[pl.ds(r, 1), :], sem)

    def start(r, c):
        copy(r).start()
        return c

    def wait(r, c):
        copy(r).wait()
        return c

    lax.fori_loop(0, n_real, start, 0)
    lax.fori_loop(0, n_real, wait, 0)
    o_ref[...] = buf[...].astype(o_ref.dtype)


def _row_gather(src, idx, n_real, rows, dtype):
    n = idx.shape[0]
    d = src.shape[1]
    return pl.pallas_call(
        functools.partial(_row_gather_kernel, rows=rows),
        grid_spec=pltpu.PrefetchScalarGridSpec(
            num_scalar_prefetch=2,
            grid=(n // rows,),
            in_specs=[pl.BlockSpec(memory_space=pl.ANY)],
            out_specs=pl.BlockSpec((rows, d), lambda i, ix, cn: (i, 0)),
            scratch_shapes=[pltpu.VMEM((rows, d), src.dtype), pltpu.SemaphoreType.DMA(())],
        ),
        out_shape=jax.ShapeDtypeStruct((n, d), dtype),
        compiler_params=_cparams(("arbitrary",)),
        name="row_gather",
    )(idx, n_real, src)


GATHER_ROWS = 256
EXPERT_TM = 512
EXPERT_TF = 256
EXPERT_VMEM_LIMIT = 62 * 1024 * 1024
EXPERT_TN = 1024
N_ASSIGN = 2 * NT
EXPERT_TILES = N_ASSIGN // EXPERT_TM + N_EXPERTS
R_PAD = EXPERT_TILES * EXPERT_TM


def _expert_kernel(te_ref, tv_ref, xs_ref, wg_ref, wu_ref, wd_ref, rw_ref, o_ref):
    i = pl.program_id(0)
    j = pl.program_id(1)
    live = tv_ref[i] > 0

    @pl.when(live)
    def _():
        x = xs_ref[...]
        gate = _dot(x, wg_ref[...].astype(BF16))
        up = _dot(x, wu_ref[...].astype(BF16))
        hid = _silu(gate) * up * rw_ref[...]
        hb = hid.astype(BF16)

        @pl.when(j == 0)
        def _():
            o_ref[...] = jnp.zeros_like(o_ref)

        for c in range(D_MODEL // EXPERT_TN):
            cs = slice(c * EXPERT_TN, (c + 1) * EXPERT_TN)
            o_ref[:, cs] += _dot(hb, wd_ref[:, cs].astype(BF16))

    @pl.when(jnp.logical_not(live) & (j == 0))
    def _():
        o_ref[...] = jnp.zeros_like(o_ref)


def _experts(xs, roww, tile_expert, tile_live, w_g, w_u, w_d):
    tm, tf = EXPERT_TM, EXPERT_TF
    return pl.pallas_call(
        _expert_kernel,
        grid_spec=pltpu.PrefetchScalarGridSpec(
            num_scalar_prefetch=2,
            grid=(EXPERT_TILES, D_EXPERT // tf),
            in_specs=[pl.BlockSpec((tm, D_MODEL), lambda i, j, te, tv: (i, 0)),
                      pl.BlockSpec((None, D_MODEL, tf), lambda i, j, te, tv: (te[i], 0, j)),
                      pl.BlockSpec((None, D_MODEL, tf), lambda i, j, te, tv: (te[i], 0, j)),
                      pl.BlockSpec((None, tf, D_MODEL), lambda i, j, te, tv: (te[i], j, 0)),
                      pl.BlockSpec((tm, 1), lambda i, j, te, tv: (i, 0))],
            out_specs=pl.BlockSpec((tm, D_MODEL), lambda i, j, te, tv: (i, 0)),
        ),
        out_shape=jax.ShapeDtypeStruct((R_PAD, D_MODEL), F32),
        compiler_params=_cparams(("arbitrary", "arbitrary"), EXPERT_VMEM_LIMIT),
        name="moe_experts",
    )(tile_expert, tile_live, xs, w_g, w_u, w_d, roww)


def _combine_kernel(p0_ref, p1_ref, ys_ref, h_ref, g_ref, h2_ref, n_ref, buf0, buf1, sem, *, rows):
    base = pl.program_id(0) * rows

    def copies(r):
        return (pltpu.make_async_copy(ys_ref.at[pl.ds(p0_ref[base + r], 1), :], buf0.at[pl.ds(r, 1), :], sem.at[0]),
                pltpu.make_async_copy(ys_ref.at[pl.ds(p1_ref[base + r], 1), :], buf1.at[pl.ds(r, 1), :], sem.at[1]))

    def start(r, c):
        for cp in copies(r):
            cp.start()
        return c

    def wait(r, c):
        for cp in copies(r):
            cp.wait()
        return c

    lax.fori_loop(0, rows, start, 0)
    lax.fori_loop(0, rows, wait, 0)
    h2 = h_ref[...] + (buf0[...] + buf1[...])
    h2_ref[...] = h2
    n_ref[...] = _rms_rows(h2, g_ref[...]).astype(BF16)


def _combine(ys, pos0, pos1, h, g_ple_in, rows):
    row = lambda i, a, b: (i, 0)
    return pl.pallas_call(
        functools.partial(_combine_kernel, rows=rows),
        grid_spec=pltpu.PrefetchScalarGridSpec(
            num_scalar_prefetch=2,
            grid=(NT // rows,),
            in_specs=[pl.BlockSpec(memory_space=pl.ANY), pl.BlockSpec((rows, D_MODEL), row),
                      pl.BlockSpec((1, D_MODEL), lambda i, a, b: (0, 0))],
            out_specs=[pl.BlockSpec((rows, D_MODEL), row), pl.BlockSpec((rows, D_MODEL), row)],
            scratch_shapes=[pltpu.VMEM((rows, D_MODEL), F32), pltpu.VMEM((rows, D_MODEL), F32),
                            pltpu.SemaphoreType.DMA((2,))],
        ),
        out_shape=[jax.ShapeDtypeStruct((NT, D_MODEL), F32), jax.ShapeDtypeStruct((NT, D_MODEL), BF16)],
        compiler_params=_cparams(("arbitrary",)),
        name="moe_combine",
    )(pos0, pos1, ys, h, g_ple_in.reshape(1, D_MODEL))


def _dispatch_plan(route):
    tm = EXPERT_TM
    a_exp = route[:, 0:2].astype(jnp.int32).reshape(-1)
    a_w = route[:, 2:4].reshape(-1)
    a_tok = jnp.arange(N_ASSIGN, dtype=jnp.int32) // 2
    order = jnp.argsort(a_exp, stable=True)
    counts = jnp.bincount(a_exp, length=N_EXPERTS).astype(jnp.int32)
    padded = -(-counts // tm) * tm
    pad_end = jnp.cumsum(padded)
    pad_start = pad_end - padded
    start = jnp.cumsum(counts) - counts
    sorted_rank = jnp.argsort(order).astype(jnp.int32)
    pos = pad_start[a_exp] + sorted_rank - start[a_exp]
    tile_row = jnp.arange(EXPERT_TILES, dtype=jnp.int32) * tm
    tile_live = (tile_row < pad_end[-1]).astype(jnp.int32)
    tile_expert = jnp.minimum(jnp.searchsorted(pad_end, tile_row, side="right"), N_EXPERTS - 1).astype(jnp.int32)
    slot = jnp.arange(R_PAD, dtype=jnp.int32)
    e_s = tile_expert[slot // tm]
    off = slot - pad_start[e_s]
    real = (off < counts[e_s]) & (slot < pad_end[-1])
    a_s = order[jnp.clip(start[e_s] + off, 0, N_ASSIGN - 1)]
    src = jnp.where(real, a_tok[a_s], 0).astype(jnp.int32)
    roww = jnp.where(real, a_w[a_s], 0.0)
    chunk_real = jnp.sum(real.reshape(R_PAD // GATHER_ROWS, GATHER_ROWS), axis=1).astype(jnp.int32)
    last_live = jnp.maximum(pad_end[-1] // tm - 1, 0)
    tile_expert = jnp.where(tile_live > 0, tile_expert, tile_expert[last_live])
    pos2 = pos.reshape(NT, 2)
    return src, chunk_real, roww.reshape(R_PAD, 1), tile_expert, tile_live, pos2[:, 0], pos2[:, 1]


def _sample_rows(sample):
    return jnp.pad(sample, ((0, 0), (0, SROW - DEC_SEQ), (0, 0))).reshape(NS, sample.shape[-1])


def _rows_of(prompt, sample):
    return jnp.concatenate([prompt.reshape(NP, prompt.shape[-1]), _sample_rows(sample)], axis=0)


def _kv_rows(z_p, z_s, col):
    prompt = z_p[:, col:col + KV_ROW].reshape(1, BATCH, SEQ, NSA_KV_HEADS, 2, HEAD_DIM)
    sample = z_s[:, col:col + KV_ROW].reshape(DEC_BATCH, SROW, NSA_KV_HEADS, 2, HEAD_DIM)[None, :, :DEC_SEQ]
    return prompt, sample


def kernel(x_prompt, x_sample, cache_cmp, cache_sel, cache_win, state_hgrn, page_table, p_prompt, p_sample, g_attn, w_in, lb_logits, g_lin_out, g_q, g_kc, g_ks, g_kw, cmp_pos_logits, cmp_w1, cmp_w2, g_nsa_out, w_out, g_ffn, w_group_router, b_group_router, w_expert_router, b_expert_router, w_exp_gate, w_exp_up, w_exp_down, g_ple_in, w_ple_gate, w_ple_proj, g_ple_out):
    l = 0
    x_p = x_prompt.reshape(NP, D_MODEL)
    x_s = _sample_rows(x_sample)
    p_rows = _rows_of(p_prompt[l], p_sample[l])

    a_p = _rms_cast(x_p, g_attn[l], 512, BF16)
    a_s = _rms_cast(x_s, g_attn[l], NS, F32)
    tn = 512
    ones = jnp.ones((HEAD_DIM,), F32)
    kv_gain = lambda gk: jnp.tile(jnp.concatenate([gk, ones]), NSA_KV_HEADS)
    kv_sel = jnp.tile(jnp.concatenate([ones, 0 * ones]), NSA_KV_HEADS)
    gain = jnp.concatenate([jnp.ones((COL_NQ,), F32), jnp.tile(g_q[l], NSA_HEADS), jnp.ones((KV_ROW,), F32),
                            kv_gain(g_ks[l]), kv_gain(g_kw[l])]).reshape(1, N_MAIN)
    normsel = jnp.concatenate([jnp.zeros((COL_NQ,), F32), jnp.ones((NSA_WIDTH,), F32), jnp.zeros((KV_ROW,), F32),
                               kv_sel, kv_sel]).reshape(1, N_MAIN)
    tile_norm = np.zeros((N_MAIN // tn,), np.int32)
    tile_norm[COL_NQ // tn:COL_KVC // tn] = 1
    tile_norm[COL_KVS // tn:] = 1
    flags = jnp.asarray(tile_norm)
    z_p = _matmul(a_p, w_in[l], N_MAIN, 1024, tn, "groupnorm", (gain, normsel), ("row", "row"), flags=flags,
                  name="in_proj")
    z_s = _matmul(a_s, w_in[l], N_MAIN, NS, tn, "groupnorm", (gain, normsel), ("row", "row"), flags=flags,
                  name="in_proj")
    w_gate = jnp.pad(w_in[l][:, COL_GATE:COL_GATE + N_GATE], ((0, 0), (0, 128 - N_GATE)))
    gates_p = _matmul(a_p, w_gate, 128, 1024, 128, "sigmoid", name="gate_proj")
    gates_s = _matmul(a_s, w_gate, 128, NS, 128, "sigmoid", name="gate_proj")

    lb = jnp.cumsum(jax.nn.softmax(lb_logits.astype(F32), axis=0), axis=0)[l]
    ylin_p, st_p = _hgrn(z_p, lb, g_lin_out[l], None, batch=BATCH, rows=SEQ, c=128, t_valid=128, hp=2,
                         precise=False)
    ylin_s, st_s = _hgrn(z_s, lb, g_lin_out[l], state_hgrn[l], batch=DEC_BATCH, rows=SROW, c=SROW,
                         t_valid=DEC_SEQ, hp=LIN_HEADS, precise=True)

    cmp_p, cmp_s = _kv_rows(z_p, z_s, COL_KVC)
    sel_p, sel_s = _kv_rows(z_p, z_s, COL_KVS)
    win_p, win_new = _kv_rows(z_p, z_s, COL_KVW)
    logits_x = jnp.broadcast_to(cmp_pos_logits[l].transpose(2, 0, 1).reshape(CMP_LEN, KV_SUB, 1),
                                (CMP_LEN, KV_SUB, HEAD_DIM))
    pt_flat = page_table.reshape(-1)
    ckv_p = _compress_prompt(cmp_p.reshape(BATCH, SEQ, KV_SUB, HEAD_DIM), logits_x, cmp_w1[l], cmp_w2[l], g_kc[l])
    new_cmp = z_s[:, COL_KVC:COL_KVC + KV_ROW].reshape(DEC_BATCH, SROW, KV_SUB, HEAD_DIM)
    ckv_s = _compress_sample(cache_cmp[l].reshape(-1, PAGE_SIZE, KV_SUB, HEAD_DIM), pt_flat, new_cmp, logits_x,
                             cmp_w1[l], cmp_w2[l], g_kc[l])
    ocmp_p, mask_p = _cmp_select(z_p, ckv_p.reshape(BATCH, -1, KV_ROW), batch=BATCH, rows=SEQ, tq=256,
                                 nb=SEQ // SEL_BLOCK, pos_base=0, want_idx=False, precise=False)
    nb_s = -(-(PAST_LEN + DEC_SEQ) // SEL_BLOCK)
    ocmp_s, _, idx_s = _cmp_select(z_s, ckv_s.reshape(DEC_BATCH, -1, KV_ROW), batch=DEC_BATCH, rows=SROW,
                                   tq=SROW, nb=nb_s, pos_base=PAST_LEN, want_idx=True, precise=True)
    osel_p = _flash_prompt(z_p, "sel", mask_p)
    owin_p = _flash_prompt(z_p, "win")
    idx = idx_s.reshape(DEC_BATCH, SROW, NSA_KV_HEADS, HEAD_DIM)[:, :DEC_SEQ, :, :N_SEL].astype(jnp.int32)
    blk_no = jnp.concatenate([idx[..., :2], idx[..., 3:]], axis=-1)
    page = jnp.take_along_axis(page_table[:, None, None, :], blk_no // 2, axis=-1)
    phys_half = page * 2 + blk_no % 2
    osel_s = _sel_sample(z_s, cache_sel[l], phys_half.reshape(-1), blk_no.reshape(-1))
    owin_s = _win_sample(z_s, cache_win[l].reshape(DEC_BATCH, -1, HEAD_DIM))

    y_p = _mix(ylin_p, ocmp_p, osel_p, owin_p, gates_p, g_nsa_out[l], 256, BF16)
    y_s = _mix(ylin_s, ocmp_s, osel_s, owin_s, gates_s, g_nsa_out[l], 256, F32)
    attn_p = _matmul(y_p, w_out[l], D_MODEL, 1024, 512, "none", name="out_proj")
    attn_s = _matmul(y_s, w_out[l], D_MODEL, NS, 512, "none", name="out_proj")

    w_route = jnp.pad(jnp.concatenate([w_group_router[l], w_expert_router[l]], axis=1),
                      ((0, 0), (0, 128 - N_GROUPS - N_EXPERTS)))
    b_route = jnp.pad(jnp.concatenate([b_group_router[l], b_expert_router[l]]),
                      (0, 128 - N_GROUPS - N_EXPERTS)).reshape(1, 128)
    h1, m, route = _router(x_p, x_s, attn_p, attn_s, g_ffn[l], w_route, b_route, 128)
    src, chunk_real, roww, tile_expert, tile_live, pos0, pos1 = _dispatch_plan(route)
    xs = _row_gather(m, src, chunk_real, GATHER_ROWS, BF16)
    ys = _experts(xs, roww, tile_expert, tile_live, w_exp_gate[l], w_exp_up[l], w_exp_down[l])
    h2, n2 = _combine(ys, pos0, pos1, h1, g_ple_in[l], 128)

    pp = _matmul(p_rows.astype(BF16), w_ple_proj[l], D_MODEL, 256, D_MODEL, "rowrms",
                 (g_ple_out[l].reshape(1, D_MODEL),), ("row",), name="ple_proj")
    out_p = _matmul(n2, w_ple_gate[l], D_MODEL, 1024, 512, "ple", (h2, pp), ("full", "full"), name="ple_gate",
                    row0=0, m=NP)
    out_s = _matmul(n2, w_ple_gate[l], D_MODEL, NS, 512, "ple", (h2, pp), ("full", "full"), name="ple_gate",
                    row0=NP, m=NS)

    y_prompt = out_p.reshape(BATCH, SEQ, D_MODEL)
    y_sample = out_s.reshape(DEC_BATCH, SROW, D_MODEL)[:, :DEC_SEQ]
    w_keep = cache_win.shape[2]
    win_prompt = win_p[:, :, SEQ - min(WINDOW, SEQ):]
    win_sample = jnp.concatenate([cache_win[l], win_new[0]], axis=1)[None, :, DEC_SEQ:DEC_SEQ + w_keep]
    return (y_prompt, y_sample, cmp_p, cmp_s, sel_p, sel_s, win_prompt, win_sample,
            st_p[None], st_s[None])
```

```python
import functools
import math

import numpy as np
import jax
import jax.numpy as jnp
from jax import lax
from jax.experimental import pallas as pl
from jax.experimental.pallas import tpu as pltpu

F32 = jnp.float32
BF16 = jnp.bfloat16

D_MODEL = 4096
BATCH = 4
SEQ = 2048
DEC_BATCH = 32
DEC_SEQ = 4
PAST_LEN = 8192
PAGE_SIZE = 128
N_PAGES = PAST_LEN // PAGE_SIZE
LIN_WIDTH = 2048
NSA_WIDTH = 2048
HEAD_DIM = 128
LIN_HEADS = 16
NSA_HEADS = 16
NSA_KV_HEADS = 4
NSA_REP = 4
KV_WIDTH = NSA_KV_HEADS * HEAD_DIM
KV_ROW = 2 * KV_WIDTH
KV_SUB = 2 * NSA_KV_HEADS
CMP_STRIDE = 16
CMP_LEN = 32
CMP_HIDDEN = 256
SEL_BLOCK = 64
N_SEL = 16
WINDOW = 512
N_EXPERTS = 16
N_GROUPS = 4
EXP_PER_GROUP = 4
D_EXPERT = 1024
PLE_DIM = 256
EPS = 1e-6
NEG = -1e30
FORCE = 1e4
SCALE = HEAD_DIM ** -0.5

COL_LQ, COL_LF, COL_LI, COL_LG = 0, 2048, 4096, 6144
COL_NQ = 8192
COL_KVC, COL_KVS, COL_KVW = 10240, 11264, 12288
COL_GATE = 13312
N_MAIN = 13312
N_GATE = 3 * NSA_HEADS

NP = BATCH * SEQ
SROW = 16
NS = DEC_BATCH * SROW
NT = NP + NS

VMEM_LIMIT = 56 * 1024 * 1024

SLOPES = [2.0 ** (-8.0 * (h + 1) / NSA_HEADS) for h in range(NSA_HEADS)]


def _cparams(sem, vmem_limit=VMEM_LIMIT):
    return pltpu.CompilerParams(dimension_semantics=sem, vmem_limit_bytes=vmem_limit)


def _dot(a, b):
    return jnp.dot(a, b, preferred_element_type=F32)


def _dot_nt(a, b):
    return lax.dot_general(a, b, (((1,), (1,)), ((), ())), preferred_element_type=F32)


def _dot_tn(a, b):
    return lax.dot_general(a, b, (((0,), (0,)), ((), ())), preferred_element_type=F32)


def _split2(x):
    hi = x.astype(BF16)
    return hi, (x - hi.astype(F32)).astype(BF16)


def _dot3(a, b, form):
    a_hi, a_lo = _split2(a)
    b_hi, b_lo = _split2(b)
    if form == "tn":
        m = a.shape[1]
        two = _dot_tn(jnp.concatenate([a_hi, a_lo], axis=1), b_hi)
        return two[:m] + two[m:] + _dot_tn(a_hi, b_lo)
    f = _dot_nt if form == "nt" else _dot
    m = a.shape[0]
    two = f(jnp.concatenate([a_hi, a_lo], axis=0), b_hi)
    return two[:m] + two[m:] + f(a_hi, b_lo)


class _Mxu:
    def __init__(self, precise):
        self.precise = precise

    def op(self, x):
        return x.astype(F32) if self.precise else x.astype(BF16)

    def dot(self, a, b):
        return _dot3(a, b, "nn") if self.precise else _dot(a, b)

    def dot_nt(self, a, b):
        return _dot3(a, b, "nt") if self.precise else _dot_nt(a, b)

    def dot_tn(self, a, b):
        return _dot3(a, b, "tn") if self.precise else _dot_tn(a, b)


def _split3(x):
    a = x.astype(BF16)
    r = x - a.astype(F32)
    b = r.astype(BF16)
    c = (r - b.astype(F32)).astype(BF16)
    return a, b, c


def _rms_rows(x, g):
    ms = jnp.mean(x * x, axis=-1, keepdims=True)
    return x * lax.rsqrt(ms + EPS) * g


def _silu(x):
    return x * jax.nn.sigmoid(x)


def _two_group_specs(tm, width):
    npt = NP // tm
    return [pl.BlockSpec((tm, width), lambda i: (jnp.minimum(i, npt - 1), 0)),
            pl.BlockSpec((tm, width), lambda i: (jnp.maximum(i - npt, 0), 0))]


def _pick_group(p_ref, s_ref, tm):
    return jnp.where(pl.program_id(0) < NP // tm, p_ref[...], s_ref[...])


def _rms_cast_kernel(x_ref, g_ref, o_ref):
    o_ref[...] = _rms_rows(x_ref[...], g_ref[...]).astype(o_ref.dtype)


def _rms_cast(x, g, tm, dtype):
    n, d = x.shape
    return pl.pallas_call(
        _rms_cast_kernel,
        grid=(n // tm,),
        in_specs=[pl.BlockSpec((tm, d), lambda i: (i, 0)), pl.BlockSpec((1, d), lambda i: (0, 0))],
        out_specs=pl.BlockSpec((tm, d), lambda i: (i, 0)),
        out_shape=jax.ShapeDtypeStruct((n, d), dtype),
        compiler_params=_cparams(("arbitrary",)),
        name="rms_cast",
    )(x, g.reshape(1, d))


def _mm_kernel(flag_ref, x_ref, w_ref, *rest, epilogue, n_extra, precise):
    extras = rest[:n_extra]
    o_ref = rest[n_extra]
    wb_ref = rest[n_extra + 1]
    j = pl.program_id(0)

    @pl.when(pl.program_id(1) == 0)
    def _():
        w = w_ref[...]
        hi = w.astype(BF16)
        wb_ref[0] = hi
        if precise:
            wb_ref[1] = (w - hi.astype(F32)).astype(BF16)

    if precise:
        x = x_ref[...]
        x_hi = x.astype(BF16)
        x_lo = (x - x_hi.astype(F32)).astype(BF16)
        acc = _dot(x_hi, wb_ref[0]) + (_dot(x_hi, wb_ref[1]) + _dot(x_lo, wb_ref[0]))
    else:
        acc = _dot(x_ref[...], wb_ref[0])
    if epilogue == "groupnorm":
        gain_ref, sel_ref = extras
        normed = flag_ref[j] > 0

        @pl.when(normed)
        def _():
            for c in range(acc.shape[1] // HEAD_DIM):
                cs = slice(c * HEAD_DIM, (c + 1) * HEAD_DIM)
                blk = acc[:, cs]
                o_ref[:, cs] = jnp.where(sel_ref[:, cs] > 0, _rms_rows(blk, gain_ref[:, cs]), blk)

        @pl.when(jnp.logical_not(normed))
        def _():
            o_ref[...] = acc
    elif epilogue == "sigmoid":
        o_ref[...] = jax.nn.sigmoid(acc)
    elif epilogue == "none":
        o_ref[...] = acc
    elif epilogue == "rowrms":
        o_ref[...] = _rms_rows(acc, extras[0][...])
    elif epilogue == "ple":
        h_ref, pp_ref = extras
        o_ref[...] = h_ref[...] + jax.nn.sigmoid(acc) * pp_ref[...]
    else:
        raise ValueError(epilogue)


def _matmul(x, w, n_out, tm, tn, epilogue, extras=(), extra_kinds=(), flags=None, name="mm", row0=0, m=None):
    precise = x.dtype == F32
    k = x.shape[1]
    m = x.shape[0] if m is None else m
    blk0 = row0 // tm
    n_tiles = n_out // tn
    if flags is None:
        flags = jnp.zeros((n_tiles,), jnp.int32)
    specs = [pl.BlockSpec((tm, k), lambda j, i, f: (blk0 + i, 0)), pl.BlockSpec((k, tn), lambda j, i, f: (0, j))]
    for kind in extra_kinds:
        if kind == "row":
            specs.append(pl.BlockSpec((1, tn), lambda j, i, f: (0, j)))
        else:
            specs.append(pl.BlockSpec((tm, tn), lambda j, i, f: (blk0 + i, j)))
    return pl.pallas_call(
        functools.partial(_mm_kernel, epilogue=epilogue, n_extra=len(extras), precise=precise),
        grid_spec=pltpu.PrefetchScalarGridSpec(
            num_scalar_prefetch=1,
            grid=(n_tiles, m // tm),
            in_specs=specs,
            out_specs=pl.BlockSpec((tm, tn), lambda j, i, f: (i, j)),
            scratch_shapes=[pltpu.VMEM((2 if precise else 1, k, tn), BF16)],
        ),
        out_shape=jax.ShapeDtypeStruct((m, n_out), F32),
        compiler_params=_cparams(("arbitrary", "arbitrary")),
        name=name,
    )(flags, x, w, *extras)


def _hgrn_masks(c):
    levels = int(math.log2(c))
    sums = np.zeros((levels + 2, c, c), np.float32)
    pair = np.zeros((levels + 1, c, c), np.float32)
    pair[0] = np.eye(c)
    for lv in range(levels):
        h = 1 << lv
        for t in range(c):
            base = (t // (2 * h)) * 2 * h
            m = base + h - 1
            if t % (2 * h) >= h:
                sums[lv, t, m + 1:t + 1] = 1.0
                pair[lv + 1, t, base:base + h] = 1.0
            else:
                sums[lv, t, t + 1:m + 1] = 1.0
    for t in range(c):
        sums[levels, t, :t + 1] = 1.0
        sums[levels + 1, t, t + 1:] = 1.0
    return sums.reshape((levels + 2) * c, c), pair


def _hgrn_kernel(q_ref, f_ref, v_ref, g_ref, lb_ref, gout_ref, sums_ref, pair_ref, *rest, c, n_chunks, t_valid,
                 hp, has_s0, precise):
    mx = _Mxu(precise)
    y_ref, s_ref, st_scr = rest[-3:]
    for hh in range(hp):
        st_scr[hh] = rest[0][hh].T if has_s0 else jnp.zeros((HEAD_DIM, HEAD_DIM), F32)
    levels = int(math.log2(c))
    row = lax.broadcasted_iota(jnp.int32, (c, 1), 0)

    def chunk(ci, carry):
        r0 = pl.multiple_of(ci * c, c)
        for hh in range(hp):
            hs = slice(hh * HEAD_DIM, (hh + 1) * HEAD_DIM)
            lb = lb_ref[hh]
            q = q_ref[pl.ds(r0, c), hs]
            z = f_ref[pl.ds(r0, c), hs]
            v = v_ref[pl.ds(r0, c), hs]
            logf = jnp.log(lb + (1.0 - lb) * jax.nn.sigmoid(z))
            k = (1.0 - lb) * jax.nn.sigmoid(-z)
            if t_valid < c:
                live = row < t_valid
                logf = jnp.where(live, logf, 0.0)
                k = jnp.where(live, k, 0.0)
            parts = jnp.concatenate(_split3(logf), axis=1)
            r = _dot(sums_ref[...], parts)
            ex = jnp.exp(r[:, :HEAD_DIM] + r[:, HEAD_DIM:2 * HEAD_DIM] + r[:, 2 * HEAD_DIM:])
            attn = pair_ref[0] * mx.dot_nt(mx.op(q), mx.op(k))
            for lv in range(levels):
                w = ex[lv * c:(lv + 1) * c]
                attn = attn + pair_ref[lv + 1] * mx.dot_nt(mx.op(q * w), mx.op(k * w))
            e_cum = ex[levels * c:(levels + 1) * c]
            e_rev = ex[(levels + 1) * c:(levels + 2) * c]
            st = st_scr[hh]
            vb = mx.op(v)
            o = mx.dot_nt(mx.op(q * e_cum), mx.op(st)) + mx.dot(mx.op(attn), vb)
            st_scr[hh] = st * e_cum[c - 1:c, :] + mx.dot_tn(vb, mx.op(k * e_rev))
            y_ref[pl.ds(r0, c), hs] = _rms_rows(o, gout_ref[...]) * _silu(g_ref[pl.ds(r0, c), hs])
        return carry

    lax.fori_loop(0, n_chunks, chunk, 0)
    for hh in range(hp):
        s_ref[hh] = st_scr[hh].T


def _hgrn(z, lb, g_lin_out, s0, *, batch, rows, c, t_valid, hp, precise):
    sums, pair = _hgrn_masks(c)
    has_s0 = s0 is not None
    w = hp * HEAD_DIM

    def zspec(col):
        cb = col // w
        return pl.BlockSpec((rows, w), lambda b, h: (b, cb + h))

    const2 = lambda b, h: (0, 0)
    st_spec = pl.BlockSpec((None, hp, HEAD_DIM, HEAD_DIM), lambda b, h: (b, h, 0, 0))
    in_specs = [zspec(COL_LQ), zspec(COL_LF), zspec(COL_LI), zspec(COL_LG),
                pl.BlockSpec((hp, 1, HEAD_DIM), lambda b, h: (h, 0, 0)),
                pl.BlockSpec((1, HEAD_DIM), const2),
                pl.BlockSpec(sums.shape, const2),
                pl.BlockSpec(pair.shape, lambda b, h: (0, 0, 0))]
    args = [z, z, z, z, lb.reshape(LIN_HEADS, 1, HEAD_DIM), g_lin_out.reshape(1, HEAD_DIM),
            jnp.asarray(sums, BF16), jnp.asarray(pair, F32)]
    if has_s0:
        in_specs.append(st_spec)
        args.append(s0)
    return pl.pallas_call(
        functools.partial(_hgrn_kernel, c=c, n_chunks=rows // c, t_valid=t_valid, hp=hp, has_s0=has_s0,
                          precise=precise),
        grid=(batch, LIN_HEADS // hp),
        in_specs=in_specs,
        out_specs=[pl.BlockSpec((rows, w), lambda b, h: (b, h)), st_spec],
        out_shape=[jax.ShapeDtypeStruct((batch * rows, LIN_WIDTH), F32),
                   jax.ShapeDtypeStruct((batch, LIN_HEADS, HEAD_DIM, HEAD_DIM), F32)],
        scratch_shapes=[pltpu.VMEM((hp, HEAD_DIM, HEAD_DIM), F32)],
        compiler_params=_cparams(("arbitrary", "arbitrary")),
        name="hgrn2",
    )(*args)


def _pool_weights(logit_ref):
    lg = logit_ref[...]
    e = jnp.exp(lg - jnp.max(lg, axis=0, keepdims=True))
    a = e / jnp.sum(e, axis=0, keepdims=True)
    return a[:CMP_STRIDE], a[CMP_STRIDE:]


def _pool(rows, a1, a2):
    n = rows.shape[0] // CMP_STRIDE
    r = rows.reshape(n, CMP_STRIDE, KV_SUB, HEAD_DIM)
    return jnp.sum(r * a1[None], axis=1), jnp.sum(r * a2[None], axis=1)


def _cmp_mlp(pooled, w1_ref, w2_ref, gkc_ref, precise):
    mx = _Mxu(precise)
    r = pooled.shape[0]
    xs = pooled.reshape(r * KV_SUB, HEAD_DIM)
    xb = mx.op(xs)
    outs = []
    for t in range(2):
        hid = _silu(mx.dot(xb, mx.op(w1_ref[t])))
        outs.append(xs + mx.dot(mx.op(hid), mx.op(w2_ref[t])))
    is_key = (lax.broadcasted_iota(jnp.int32, (r * KV_SUB, 1), 0) & 1) == 0
    out = jnp.where(is_key, _rms_rows(outs[0], gkc_ref[...]), outs[1])
    return out.reshape(r, KV_SUB, HEAD_DIM)


def _compress_prompt_kernel(kv_ref, logit_ref, w1_ref, w2_ref, gkc_ref, o_ref):
    a1, a2 = _pool_weights(logit_ref)
    p1, p2 = _pool(kv_ref[...], a1, a2)
    pooled = p1 + jnp.concatenate([p2[1:], p2[:1]], axis=0)
    o_ref[...] = _cmp_mlp(pooled, w1_ref, w2_ref, gkc_ref, precise=False)


def _compress_prompt(kv, logits_x, w1, w2, g_kc):
    nblk = SEQ // CMP_STRIDE
    cst = lambda b: (0, 0)
    cst3 = lambda b: (0, 0, 0)
    return pl.pallas_call(
        _compress_prompt_kernel,
        grid=(BATCH,),
        in_specs=[pl.BlockSpec((None, SEQ, KV_SUB, HEAD_DIM), lambda b: (b, 0, 0, 0)),
                  pl.BlockSpec((CMP_LEN, KV_SUB, HEAD_DIM), cst3),
                  pl.BlockSpec(w1.shape, cst3), pl.BlockSpec(w2.shape, cst3),
                  pl.BlockSpec((1, HEAD_DIM), cst)],
        out_specs=pl.BlockSpec((None, nblk, KV_SUB, HEAD_DIM), lambda b: (b, 0, 0, 0)),
        out_shape=jax.ShapeDtypeStruct((BATCH, nblk, KV_SUB, HEAD_DIM), F32),
        compiler_params=_cparams(("arbitrary",)),
        name="compress_prompt",
    )(kv, logits_x, w1, w2, g_kc.reshape(1, HEAD_DIM))


PAGES_PER_STEP = 16


def _compress_sample_kernel(pt_ref, *refs):
    pages = refs[:PAGES_PER_STEP]
    nxt_ref, new_ref, logit_ref, w1_ref, w2_ref, gkc_ref, o_ref = refs[PAGES_PER_STEP:]
    a1, a2 = _pool_weights(logit_ref)
    last = pl.program_id(1) == pl.num_programs(1) - 1
    nxt = jnp.where(last, new_ref[...], nxt_ref[...])
    tail = jnp.sum(nxt * a2, axis=0, keepdims=True)
    pooled = [None] * PAGES_PER_STEP
    for kk in reversed(range(PAGES_PER_STEP)):
        p1, p2 = _pool(pages[kk][...], a1, a2)
        pooled[kk] = p1 + jnp.concatenate([p2[1:], tail], axis=0)
        tail = p2[:1]
    o_ref[...] = _cmp_mlp(jnp.concatenate(pooled, axis=0), w1_ref, w2_ref, gkc_ref, precise=True)


def _compress_sample(cache, pt_flat, new_rows, logits_x, w1, w2, g_kc):
    steps = N_PAGES // PAGES_PER_STEP
    rows = PAGES_PER_STEP * PAGE_SIZE // CMP_STRIDE
    cst = lambda b, s, pt: (0, 0)
    cst3 = lambda b, s, pt: (0, 0, 0)

    def page_spec(kk):
        return pl.BlockSpec((None, PAGE_SIZE, KV_SUB, HEAD_DIM),
                            lambda b, s, pt: (pt[b * N_PAGES + s * PAGES_PER_STEP + kk], 0, 0, 0))

    nxt_spec = pl.BlockSpec(
        (None, CMP_STRIDE, KV_SUB, HEAD_DIM),
        lambda b, s, pt: (pt[b * N_PAGES + jnp.minimum((s + 1) * PAGES_PER_STEP, N_PAGES - 1)], 0, 0, 0))
    new_spec = pl.BlockSpec((None, CMP_STRIDE, KV_SUB, HEAD_DIM), lambda b, s, pt: (b, 0, 0, 0))
    nc = PAST_LEN // CMP_STRIDE
    return pl.pallas_call(
        _compress_sample_kernel,
        grid_spec=pltpu.PrefetchScalarGridSpec(
            num_scalar_prefetch=1,
            grid=(DEC_BATCH, steps),
            in_specs=[page_spec(kk) for kk in range(PAGES_PER_STEP)] + [
                nxt_spec, new_spec, pl.BlockSpec((CMP_LEN, KV_SUB, HEAD_DIM), cst3),
                pl.BlockSpec(w1.shape, cst3), pl.BlockSpec(w2.shape, cst3), pl.BlockSpec((1, HEAD_DIM), cst)],
            out_specs=pl.BlockSpec((None, rows, KV_SUB, HEAD_DIM), lambda b, s, pt: (b, s, 0, 0)),
        ),
        out_shape=jax.ShapeDtypeStruct((DEC_BATCH, nc, KV_SUB, HEAD_DIM), F32),
        compiler_params=_cparams(("arbitrary", "arbitrary")),
        name="compress_sample",
    )(pt_flat, *([cache] * (PAGES_PER_STEP + 1)), new_rows, logits_x, w1, w2, g_kc.reshape(1, HEAD_DIM))


def _overlap_matrix(ncp, nbp, nb):
    c0 = np.arange(ncp)[:, None] * CMP_STRIDE
    b0 = np.arange(nbp)[None, :] * SEL_BLOCK
    ov = (c0 < b0 + SEL_BLOCK) & (c0 + CMP_LEN > b0) & (np.arange(nbp)[None, :] < nb)
    return ov.astype(np.float32)


def _cmp_select_kernel(q_ref, ckv_ref, ov_ref, o_ref, m_ref, *maybe_idx, tq, nb, pos_base, tiled, precise):
    mx = _Mxu(precise)
    ncp = ckv_ref.shape[0]
    nbp = ov_ref.shape[1]
    t0 = pos_base + (pl.program_id(1) * tq if tiled else 0)
    qp = t0 + lax.broadcasted_iota(jnp.int32, (tq, 1), 0)
    cend = lax.broadcasted_iota(jnp.int32, (1, ncp), 1) * CMP_STRIDE + (CMP_LEN - 1)
    valid = qp >= cend
    validf = valid.astype(F32)
    dist = (qp - cend).astype(F32)
    n_i = lax.broadcasted_iota(jnp.int32, (1, nbp), 1)
    qblk = lax.shift_right_arithmetic(qp, int(math.log2(SEL_BLOCK)))
    forced = (n_i == 0) | (n_i == qblk) | (n_i == qblk - 1)
    future = n_i * SEL_BLOCK > qp
    lane = lax.broadcasted_iota(jnp.int32, (1, HEAD_DIM), 1)
    for g in range(NSA_KV_HEADS):
        gs = slice(g * HEAD_DIM, (g + 1) * HEAD_DIM)
        kcg = mx.op(ckv_ref[:, 2 * g * HEAD_DIM:(2 * g + 1) * HEAD_DIM])
        vcg = mx.op(ckv_ref[:, (2 * g + 1) * HEAD_DIM:(2 * g + 2) * HEAD_DIM])
        q4 = mx.op(jnp.concatenate([q_ref[:, (g * NSA_REP + r) * HEAD_DIM:(g * NSA_REP + r + 1) * HEAD_DIM]
                                    for r in range(NSA_REP)], axis=0))
        s4 = mx.dot_nt(q4, kcg) * SCALE
        psum = jnp.zeros((tq, ncp), F32)
        for r in range(NSA_REP):
            h = g * NSA_REP + r
            s = jnp.where(valid, s4[r * tq:(r + 1) * tq] - SLOPES[h] * dist, NEG)
            e = jnp.exp(s - jnp.max(s, axis=-1, keepdims=True))
            p = e / jnp.sum(e, axis=-1, keepdims=True) * validf
            o_ref[:, h * HEAD_DIM:(h + 1) * HEAD_DIM] = mx.dot(mx.op(p), vcg)
            psum = psum + p
        ov = ov_ref[...]
        imp = sum(_dot(part, ov) for part in _split3(psum))
        score = jnp.where(forced, FORCE, jnp.where(future, -1.0, imp))
        score = jnp.where(n_i < nb, score, -2.0)
        rank = jnp.zeros((tq, nbp), F32)
        for m in range(nb):
            col = score[:, m:m + 1]
            ahead = (col > score) | ((col == score) & (n_i > m))
            rank = rank + jnp.where(ahead, 1.0, 0.0)
        chosen = (rank < float(N_SEL)) & (n_i < nb) & jnp.logical_not(future)
        m_ref[:, g * nbp:(g + 1) * nbp] = chosen.astype(F32)
        if maybe_idx:
            nf = n_i.astype(F32)
            idx = jnp.zeros((tq, HEAD_DIM), F32)
            for jj in range(N_SEL):
                pick = jnp.sum(jnp.where(rank == float(jj), nf, 0.0), axis=-1, keepdims=True)
                idx = jnp.where(lane == jj, pick, idx)
            maybe_idx[0][:, gs] = idx


def _cmp_select(z, ckv, *, batch, rows, tq, nb, pos_base, want_idx, precise):
    ncp = ckv.shape[1]
    nbp = -(-nb // 128) * 128
    ov = jnp.asarray(_overlap_matrix(ncp, nbp, nb), BF16)
    n_q = rows // tq
    out_specs = [pl.BlockSpec((tq, NSA_WIDTH), lambda b, i: (b * n_q + i, 0)),
                 pl.BlockSpec((tq, NSA_KV_HEADS * nbp), lambda b, i: (b * n_q + i, 0))]
    out_shape = [jax.ShapeDtypeStruct((batch * rows, NSA_WIDTH), F32),
                 jax.ShapeDtypeStruct((batch * rows, NSA_KV_HEADS * nbp), F32)]
    if want_idx:
        out_specs.append(pl.BlockSpec((tq, NSA_KV_HEADS * HEAD_DIM), lambda b, i: (b * n_q + i, 0)))
        out_shape.append(jax.ShapeDtypeStruct((batch * rows, NSA_KV_HEADS * HEAD_DIM), F32))
    return pl.pallas_call(
        functools.partial(_cmp_select_kernel, tq=tq, nb=nb, pos_base=pos_base, tiled=n_q > 1, precise=precise),
        grid=(batch, n_q),
        in_specs=[pl.BlockSpec((tq, NSA_WIDTH), lambda b, i: (b * n_q + i, COL_NQ // NSA_WIDTH)),
                  pl.BlockSpec((None, ncp, KV_ROW), lambda b, i: (b, 0, 0)),
                  pl.BlockSpec(ov.shape, lambda b, i: (0, 0))],
        out_specs=out_specs,
        out_shape=out_shape,
        compiler_params=_cparams(("arbitrary", "arbitrary")),
        name="cmp_select",
    )(z, ckv, ov)


FLASH_TQ = 512
FLASH_TK = {"sel": 1024, "win": 512}


def _flash_kernel(slope_ref, q_ref, kv_ref, *rest, mode):
    if mode == "sel":
        mask_ref, exp_ref, o_ref, m_scr, l_scr, acc_scr = rest
    else:
        o_ref, m_scr, l_scr, acc_scr = rest
    tq, tk = FLASH_TQ, FLASH_TK[mode]
    g = pl.program_id(1)
    qi = pl.program_id(2)
    kj = pl.program_id(3)
    if mode == "sel":
        ka = kj
        needed = kj * tk <= qi * tq + (tq - 1)
    else:
        ka = qi * (tq // tk) - WINDOW // tk + kj
        needed = ka >= 0

    @pl.when(kj == 0)
    def _():
        m_scr[...] = jnp.full_like(m_scr, NEG)
        l_scr[...] = jnp.zeros_like(l_scr)
        acc_scr[...] = jnp.zeros_like(acc_scr)

    @pl.when(needed)
    def _():
        q4 = jnp.concatenate([q_ref[:, r * HEAD_DIM:(r + 1) * HEAD_DIM] for r in range(NSA_REP)],
                             axis=0).astype(BF16)
        kb = kv_ref[:, :HEAD_DIM].astype(BF16)
        vb = kv_ref[:, HEAD_DIM:].astype(BF16)
        s4 = _dot_nt(q4, kb) * SCALE
        qpos = qi * tq + lax.broadcasted_iota(jnp.int32, (tq, 1), 0)
        kpos = ka * tk + lax.broadcasted_iota(jnp.int32, (1, tk), 1)
        d = qpos - kpos
        ok = d >= 0
        if mode == "sel":
            ok = ok & (_dot(mask_ref[...].astype(BF16), exp_ref[...]) > 0.5)
        else:
            ok = ok & (d < WINDOW)
        df = d.astype(F32)
        for r in range(NSA_REP):
            s = jnp.where(ok, s4[r * tq:(r + 1) * tq] - slope_ref[g * NSA_REP + r] * df, NEG)
            m_prev = m_scr[r]
            m_new = jnp.maximum(m_prev, jnp.max(s, axis=-1, keepdims=True))
            p = jnp.where(ok, jnp.exp(s - m_new), 0.0)
            alpha = jnp.exp(m_prev - m_new)
            l_scr[r] = alpha * l_scr[r] + jnp.sum(p, axis=-1, keepdims=True)
            acc_scr[r] = alpha * acc_scr[r] + _dot(p.astype(BF16), vb)
            m_scr[r] = m_new

    @pl.when(kj == pl.num_programs(3) - 1)
    def _():
        for r in range(NSA_REP):
            o_ref[:, r * HEAD_DIM:(r + 1) * HEAD_DIM] = acc_scr[r] / l_scr[r]


def _flash_prompt(z, mode, mask=None):
    tq, tk = FLASH_TQ, FLASH_TK[mode]
    n_q = SEQ // tq
    slopes = jnp.asarray(SLOPES, F32)
    qw = NSA_REP * HEAD_DIM
    kvw = 2 * HEAD_DIM
    if mode == "sel":
        n_kv = SEQ // tk
        col0 = COL_KVS // kvw
        kv_map = lambda b, g, i, j: (b * n_kv + jnp.minimum(j, (i * tq + tq - 1) // tk), col0 + g)
    else:
        n_kv = WINDOW // tk + tq // tk
        col0 = COL_KVW // kvw
        kv_map = lambda b, g, i, j: (b * (SEQ // tk) + jnp.maximum(i * (tq // tk) - WINDOW // tk + j, 0), col0 + g)
    in_specs = [pl.BlockSpec(memory_space=pltpu.SMEM),
                pl.BlockSpec((tq, qw), lambda b, g, i, j: (b * n_q + i, COL_NQ // qw + g)),
                pl.BlockSpec((tk, kvw), kv_map)]
    args = [slopes, z, z]
    if mode == "sel":
        nbp = mask.shape[1] // NSA_KV_HEADS
        blk = (np.arange(SEQ)[None, :] // SEL_BLOCK) == np.arange(nbp)[:, None]
        expand = jnp.asarray(blk.reshape(nbp, n_kv, tk).transpose(1, 0, 2).astype(np.float32), BF16)
        in_specs += [pl.BlockSpec((tq, nbp), lambda b, g, i, j: (b * n_q + i, g)),
                     pl.BlockSpec((None, nbp, tk), lambda b, g, i, j: (jnp.minimum(j, (i * tq + tq - 1) // tk), 0, 0))]
        args += [mask, expand]
    return pl.pallas_call(
        functools.partial(_flash_kernel, mode=mode),
        grid=(BATCH, NSA_KV_HEADS, n_q, n_kv),
        in_specs=in_specs,
        out_specs=pl.BlockSpec((tq, qw), lambda b, g, i, j: (b * n_q + i, g)),
        out_shape=jax.ShapeDtypeStruct((NP, NSA_WIDTH), F32),
        scratch_shapes=[pltpu.VMEM((NSA_REP, tq, 1), F32), pltpu.VMEM((NSA_REP, tq, 1), F32),
                        pltpu.VMEM((NSA_REP, tq, HEAD_DIM), F32)],
        compiler_params=_cparams(("arbitrary",) * 4),
        name="flash_" + mode,
    )(*args)


def _softmax_parts(scores, oks):
    mx = functools.reduce(jnp.maximum, [jnp.max(jnp.where(ok, s, NEG), axis=-1, keepdims=True)
                                        for s, ok in zip(scores, oks)])
    es = [jnp.where(ok, jnp.exp(jnp.where(ok, s, NEG) - mx), 0.0) for s, ok in zip(scores, oks)]
    den = functools.reduce(lambda a, b: a + b, [jnp.sum(e, axis=-1, keepdims=True) for e in es])
    return [e / den for e in es]


def _win_sample_kernel(q_ref, cache_ref, new_ref, o_ref):
    mx = _Mxu(True)
    wk = cache_ref.shape[0] // KV_SUB
    row = lax.broadcasted_iota(jnp.int32, (NSA_REP * SROW, 1), 0)
    qi = row & (SROW - 1)
    jc = lax.broadcasted_iota(jnp.int32, (1, wk), 1)
    jn = lax.broadcasted_iota(jnp.int32, (1, SROW), 1)
    d_c = qi + (wk - jc)
    d_n = qi - jn
    ok_c = d_c < WINDOW
    ok_n = (d_n >= 0) & (jn < DEC_SEQ)
    for g in range(NSA_KV_HEADS):
        kcol = slice(g * 2 * HEAD_DIM, g * 2 * HEAD_DIM + HEAD_DIM)
        vcol = slice(g * 2 * HEAD_DIM + HEAD_DIM, (g + 1) * 2 * HEAD_DIM)
        k_c = cache_ref[pl.ds(2 * g, wk, stride=KV_SUB), :]
        v_c = cache_ref[pl.ds(2 * g + 1, wk, stride=KV_SUB), :]
        q4 = jnp.concatenate([q_ref[:, (g * NSA_REP + r) * HEAD_DIM:(g * NSA_REP + r + 1) * HEAD_DIM]
                              for r in range(NSA_REP)], axis=0)
        s_c = mx.dot_nt(q4, k_c) * SCALE
        s_n = mx.dot_nt(q4, new_ref[:, kcol]) * SCALE
        slope = jnp.concatenate([jnp.full((SROW, 1), SLOPES[g * NSA_REP + r], F32) for r in range(NSA_REP)], axis=0)
        s_c = s_c - slope * d_c.astype(F32)
        s_n = s_n - slope * d_n.astype(F32)
        p_c, p_n = _softmax_parts([s_c, s_n], [ok_c, ok_n])
        o = mx.dot(p_c, v_c) + mx.dot(p_n, new_ref[:, vcol])
        for r in range(NSA_REP):
            h = g * NSA_REP + r
            o_ref[:, h * HEAD_DIM:(h + 1) * HEAD_DIM] = o[r * SROW:(r + 1) * SROW]


def _win_sample(z, cache_win):
    wk8 = cache_win.shape[1]
    return pl.pallas_call(
        _win_sample_kernel,
        grid=(DEC_BATCH,),
        in_specs=[pl.BlockSpec((SROW, NSA_WIDTH), lambda b: (b, COL_NQ // NSA_WIDTH)),
                  pl.BlockSpec((None, wk8, HEAD_DIM), lambda b: (b, 0, 0)),
                  pl.BlockSpec((SROW, KV_ROW), lambda b: (b, COL_KVW // KV_ROW))],
        out_specs=pl.BlockSpec((SROW, NSA_WIDTH), lambda b: (b, 0)),
        out_shape=jax.ShapeDtypeStruct((NS, NSA_WIDTH), F32),
        compiler_params=_cparams(("arbitrary",)),
        name="win_sample",
    )(z, cache_win, z)


N_GATHER = N_SEL - 1


def _sel_sample_kernel(phys_ref, blk_ref, slope_ref, q_ref, new_ref, *rest):
    blocks = rest[:N_GATHER]
    o_ref = rest[-1]
    b = pl.program_id(0)
    g = pl.program_id(1)
    i = pl.program_id(2)

    @pl.when(i == 0)
    def _():
        o_ref[...] = jnp.zeros_like(o_ref)

    base = ((b * DEC_SEQ + i) * NSA_KV_HEADS + g) * N_GATHER
    rr = lax.broadcasted_iota(jnp.int32, (8, 1), 0)
    mine = rr == i
    mx = _Mxu(True)
    nq = 16
    q4 = jnp.concatenate([jnp.sum(jnp.where(mine, q_ref[0:8, r * HEAD_DIM:(r + 1) * HEAD_DIM], 0.0),
                                  axis=0, keepdims=True) for r in range(NSA_REP)]
                         + [jnp.zeros((nq - NSA_REP, HEAD_DIM), F32)], axis=0)
    rq = lax.broadcasted_iota(jnp.int32, (nq, 1), 0)
    slope = jnp.zeros((nq, 1), F32)
    for r in range(NSA_REP):
        slope = jnp.where(rq == r, slope_ref[g * NSA_REP + r], slope)
    qp = PAST_LEN + i
    lane2 = lax.broadcasted_iota(jnp.int32, (1, 2 * SEL_BLOCK), 1)
    off = lax.shift_right_logical(lane2, 1)
    rows = jnp.concatenate([blocks[jj][...].reshape(2 * SEL_BLOCK, HEAD_DIM) for jj in range(N_GATHER)], axis=0)
    d = jnp.concatenate([qp - (blk_ref[base + jj] * SEL_BLOCK + off) for jj in range(N_GATHER)], axis=1)
    is_key = jnp.concatenate([(lane2 & 1) == 0] * N_GATHER, axis=1)
    jn = lax.broadcasted_iota(jnp.int32, (1, SROW), 1)
    d_n = i - jn
    scores = [mx.dot_nt(q4, rows) * SCALE - slope * d.astype(F32),
              mx.dot_nt(q4, new_ref[:, :HEAD_DIM]) * SCALE - slope * d_n.astype(F32)]
    oks = [jnp.broadcast_to((d >= 0) & is_key, scores[0].shape),
           jnp.broadcast_to((d_n >= 0) & (jn < DEC_SEQ), scores[1].shape)]
    p_old, p_new = _softmax_parts(scores, oks)
    o = mx.dot(pltpu.roll(p_old, 1, axis=1), rows) + mx.dot(p_new, new_ref[:, HEAD_DIM:])
    for r in range(NSA_REP):
        cs = slice(r * HEAD_DIM, (r + 1) * HEAD_DIM)
        o_ref[0:8, cs] = jnp.where(mine, o[r:r + 1], o_ref[0:8, cs])


def _sel_sample(z, cache_sel, phys_half, blk_no):
    kvw = 2 * HEAD_DIM
    qw = NSA_REP * HEAD_DIM
    halves = cache_sel.reshape(-1, SEL_BLOCK, NSA_KV_HEADS, 2, HEAD_DIM)

    def blk_spec(jj):
        return pl.BlockSpec(
            (None, SEL_BLOCK, None, 2, HEAD_DIM),
            lambda b, g, i, ph, bn: (ph[((b * DEC_SEQ + i) * NSA_KV_HEADS + g) * N_GATHER + jj], 0, g, 0, 0))

    return pl.pallas_call(
        _sel_sample_kernel,
        grid_spec=pltpu.PrefetchScalarGridSpec(
            num_scalar_prefetch=2,
            grid=(DEC_BATCH, NSA_KV_HEADS, DEC_SEQ),
            in_specs=[pl.BlockSpec(memory_space=pltpu.SMEM),
                      pl.BlockSpec((SROW, qw), lambda b, g, i, ph, bn: (b, COL_NQ // qw + g)),
                      pl.BlockSpec((SROW, kvw), lambda b, g, i, ph, bn: (b, COL_KVS // kvw + g))]
            + [blk_spec(jj) for jj in range(N_GATHER)],
            out_specs=pl.BlockSpec((SROW, qw), lambda b, g, i, ph, bn: (b, g)),
        ),
        out_shape=jax.ShapeDtypeStruct((NS, NSA_WIDTH), F32),
        compiler_params=_cparams(("arbitrary",) * 3),
        name="sel_sample",
    )(phys_half, blk_no, jnp.asarray(SLOPES, F32), z, z, *([halves] * N_GATHER))


def _mix_kernel(ylin_ref, cmp_ref, sel_ref, win_ref, gate_ref, gn_ref, y_ref):
    gates = gate_ref[...]
    parts = []
    for h in range(NSA_HEADS):
        hs = slice(h * HEAD_DIM, (h + 1) * HEAD_DIM)
        parts.append(gates[:, h:h + 1] * cmp_ref[:, hs]
                     + gates[:, NSA_HEADS + h:NSA_HEADS + h + 1] * sel_ref[:, hs]
                     + gates[:, 2 * NSA_HEADS + h:2 * NSA_HEADS + h + 1] * win_ref[:, hs])
    o = jnp.concatenate(parts, axis=1)
    y_ref[:, :LIN_WIDTH] = ylin_ref[...].astype(y_ref.dtype)
    y_ref[:, LIN_WIDTH:] = _rms_rows(o, gn_ref[...]).astype(y_ref.dtype)


def _mix(ylin, o_cmp, o_sel, o_win, gates, g_nsa_out, tm, dtype):
    n = ylin.shape[0]
    row = lambda i: (i, 0)
    return pl.pallas_call(
        _mix_kernel,
        grid=(n // tm,),
        in_specs=[pl.BlockSpec((tm, LIN_WIDTH), row)] + [pl.BlockSpec((tm, NSA_WIDTH), row)] * 3
        + [pl.BlockSpec((tm, 128), row), pl.BlockSpec((1, NSA_WIDTH), lambda i: (0, 0))],
        out_specs=pl.BlockSpec((tm, D_MODEL), row),
        out_shape=jax.ShapeDtypeStruct((n, D_MODEL), dtype),
        compiler_params=_cparams(("arbitrary",)),
        name="mix",
    )(ylin, o_cmp, o_sel, o_win, gates, g_nsa_out.reshape(1, NSA_WIDTH))


def _router_kernel(xp_ref, xs_ref, ap_ref, as_ref, g_ref, wr_ref, br_ref, h_ref, m_ref, r_ref, *, tm):
    h = _pick_group(xp_ref, xs_ref, tm) + _pick_group(ap_ref, as_ref, tm)
    h_ref[...] = h
    m = _rms_rows(h, g_ref[...])
    m_ref[...] = m
    logits = _dot3(m, wr_ref[...], "nn") + br_ref[...]
    lane = lax.broadcasted_iota(jnp.int32, (1, 128), 1)
    big = 1024
    is_g = lane < N_GROUPS
    gl = jnp.where(is_g, logits, -jnp.inf)
    gmax = jnp.max(gl, axis=-1, keepdims=True)
    gsel = jnp.min(jnp.where(gl == gmax, lane, big), axis=-1, keepdims=True)
    gw = 1.0 / jnp.sum(jnp.where(is_g, jnp.exp(gl - gmax), 0.0), axis=-1, keepdims=True)
    e_id = lane - N_GROUPS
    in_grp = (e_id >= 0) & (e_id < N_EXPERTS) & (lax.shift_right_arithmetic(e_id, int(math.log2(EXP_PER_GROUP))) == gsel)
    el = jnp.where(in_grp, logits, -jnp.inf)
    v1 = jnp.max(el, axis=-1, keepdims=True)
    i1 = jnp.min(jnp.where(el == v1, lane, big), axis=-1, keepdims=True)
    el2 = jnp.where(lane == i1, -jnp.inf, el)
    v2 = jnp.max(el2, axis=-1, keepdims=True)
    i2 = jnp.min(jnp.where(el2 == v2, lane, big), axis=-1, keepdims=True)
    t = jnp.exp(v2 - v1)
    w1 = gw / (1.0 + t)
    w2 = gw * t / (1.0 + t)
    out = jnp.where(lane == 0, (i1 - N_GROUPS).astype(F32), 0.0)
    out = jnp.where(lane == 1, (i2 - N_GROUPS).astype(F32), out)
    out = jnp.where(lane == 2, w1, out)
    out = jnp.where(lane == 3, w2, out)
    r_ref[...] = out


def _router(x_p, x_s, attn_p, attn_s, g_ffn, w_route, b_route, tm):
    row = lambda i: (i, 0)
    cst = lambda i: (0, 0)
    return pl.pallas_call(
        functools.partial(_router_kernel, tm=tm),
        grid=(NT // tm,),
        in_specs=_two_group_specs(tm, D_MODEL) + _two_group_specs(tm, D_MODEL) + [
                  pl.BlockSpec((1, D_MODEL), cst),
                  pl.BlockSpec((D_MODEL, 128), cst), pl.BlockSpec((1, 128), cst)],
        out_specs=[pl.BlockSpec((tm, D_MODEL), row), pl.BlockSpec((tm, D_MODEL), row),
                   pl.BlockSpec((tm, 128), row)],
        out_shape=[jax.ShapeDtypeStruct((NT, D_MODEL), F32), jax.ShapeDtypeStruct((NT, D_MODEL), F32),
                   jax.ShapeDtypeStruct((NT, 128), F32)],
        compiler_params=_cparams(("arbitrary",)),
        name="moe_router",
    )(x_p, x_s, attn_p, attn_s, g_ffn.reshape(1, D_MODEL), w_route, b_route)


def _row_gather_kernel(idx_ref, cnt_ref, src_ref, o_ref, buf, sem, *, rows):
    step = pl.program_id(0)
    base = step * rows
    n_real = cnt_ref[step]

    @pl.when(n_real < rows)
    def _():
        buf[...] = jnp.zeros_like(buf)

    def copy(r):
        return pltpu.make_async_copy(src_ref.at[pl.ds(idx_ref[base + r], 1), :], buf.at[pl.ds(r, 1), :], sem)

    def start(r, c):
        copy(r).start()
        return c

    def wait(r, c):
        copy(r).wait()
        return c

    lax.fori_loop(0, n_real, start, 0)
    lax.fori_loop(0, n_real, wait, 0)
    o_ref[...] = buf[...].astype(o_ref.dtype)


def _row_gather(src, idx, n_real, rows, dtype):
    n = idx.shape[0]
    d = src.shape[1]
    return pl.pallas_call(
        functools.partial(_row_gather_kernel, rows=rows),
        grid_spec=pltpu.PrefetchScalarGridSpec(
            num_scalar_prefetch=2,
            grid=(n // rows,),
            in_specs=[pl.BlockSpec(memory_space=pl.ANY)],
            out_specs=pl.BlockSpec((rows, d), lambda i, ix, cn: (i, 0)),
            scratch_shapes=[pltpu.VMEM((rows, d), src.dtype), pltpu.SemaphoreType.DMA(())],
        ),
        out_shape=jax.ShapeDtypeStruct((n, d), dtype),
        compiler_params=_cparams(("arbitrary",)),
        name="row_gather",
    )(idx, n_real, src)


GATHER_ROWS = 256
EXPERT_TM = 512
EXPERT_TF = 256
EXPERT_VMEM_LIMIT = 62 * 1024 * 1024
EXPERT_TN = 1024
N_ASSIGN = 2 * NT
EXPERT_TILES = N_ASSIGN // EXPERT_TM + N_EXPERTS
R_PAD = EXPERT_TILES * EXPERT_TM


def _expert_kernel(te_ref, tv_ref, xs_ref, wg_ref, wu_ref, wd_ref, rw_ref, o_ref):
    i = pl.program_id(0)
    j = pl.program_id(1)
    live = tv_ref[i] > 0

    @pl.when(live)
    def _():
        x = xs_ref[...]
        gate = _dot(x, wg_ref[...].astype(BF16))
        up = _dot(x, wu_ref[...].astype(BF16))
        hid = _silu(gate) * up * rw_ref[...]
        hb = hid.astype(BF16)

        @pl.when(j == 0)
        def _():
            o_ref[...] = jnp.zeros_like(o_ref)

        for c in range(D_MODEL // EXPERT_TN):
            cs = slice(c * EXPERT_TN, (c + 1) * EXPERT_TN)
            o_ref[:, cs] += _dot(hb, wd_ref[:, cs].astype(BF16))

    @pl.when(jnp.logical_not(live) & (j == 0))
    def _():
        o_ref[...] = jnp.zeros_like(o_ref)


def _experts(xs, roww, tile_expert, tile_live, w_g, w_u, w_d):
    tm, tf = EXPERT_TM, EXPERT_TF
    return pl.pallas_call(
        _expert_kernel,
        grid_spec=pltpu.PrefetchScalarGridSpec(
            num_scalar_prefetch=2,
            grid=(EXPERT_TILES, D_EXPERT // tf),
            in_specs=[pl.BlockSpec((tm, D_MODEL), lambda i, j, te, tv: (i, 0)),
                      pl.BlockSpec((None, D_MODEL, tf), lambda i, j, te, tv: (te[i], 0, j)),
                      pl.BlockSpec((None, D_MODEL, tf), lambda i, j, te, tv: (te[i], 0, j)),
                      pl.BlockSpec((None, tf, D_MODEL), lambda i, j, te, tv: (te[i], j, 0)),
                      pl.BlockSpec((tm, 1), lambda i, j, te, tv: (i, 0))],
            out_specs=pl.BlockSpec((tm, D_MODEL), lambda i, j, te, tv: (i, 0)),
        ),
        out_shape=jax.ShapeDtypeStruct((R_PAD, D_MODEL), F32),
        compiler_params=_cparams(("arbitrary", "arbitrary"), EXPERT_VMEM_LIMIT),
        name="moe_experts",
    )(tile_expert, tile_live, xs, w_g, w_u, w_d, roww)


def _combine_kernel(p0_ref, p1_ref, ys_ref, h_ref, g_ref, h2_ref, n_ref, buf0, buf1, sem, *, rows):
    base = pl.program_id(0) * rows

    def copies(r):
        return (pltpu.make_async_copy(ys_ref.at[pl.ds(p0_ref[base + r], 1), :], buf0.at[pl.ds(r, 1), :], sem.at[0]),
                pltpu.make_async_copy(ys_ref.at[pl.ds(p1_ref[base + r], 1), :], buf1.at[pl.ds(r, 1), :], sem.at[1]))

    def start(r, c):
        for cp in copies(r):
            cp.start()
        return c

    def wait(r, c):
        for cp in copies(r):
            cp.wait()
        return c

    lax.fori_loop(0, rows, start, 0)
    lax.fori_loop(0, rows, wait, 0)
    h2 = h_ref[...] + (buf0[...] + buf1[...])
    h2_ref[...] = h2
    n_ref[...] = _rms_rows(h2, g_ref[...]).astype(BF16)


def _combine(ys, pos0, pos1, h, g_ple_in, rows):
    row = lambda i, a, b: (i, 0)
    return pl.pallas_call(
        functools.partial(_combine_kernel, rows=rows),
        grid_spec=pltpu.PrefetchScalarGridSpec(
            num_scalar_prefetch=2,
            grid=(NT // rows,),
            in_specs=[pl.BlockSpec(memory_space=pl.ANY), pl.BlockSpec((rows, D_MODEL), row),
                      pl.BlockSpec((1, D_MODEL), lambda i, a, b: (0, 0))],
            out_specs=[pl.BlockSpec((rows, D_MODEL), row), pl.BlockSpec((rows, D_MODEL), row)],
            scratch_shapes=[pltpu.VMEM((rows, D_MODEL), F32), pltpu.VMEM((rows, D_MODEL), F32),
                            pltpu.SemaphoreType.DMA((2,))],
        ),
        out_shape=[jax.ShapeDtypeStruct((NT, D_MODEL), F32), jax.ShapeDtypeStruct((NT, D_MODEL), BF16)],
        compiler_params=_cparams(("arbitrary",)),
        name="moe_combine",
    )(pos0, pos1, ys, h, g_ple_in.reshape(1, D_MODEL))


def _dispatch_plan(route):
    tm = EXPERT_TM
    a_exp = route[:, 0:2].astype(jnp.int32).reshape(-1)
    a_w = route[:, 2:4].reshape(-1)
    a_tok = jnp.arange(N_ASSIGN, dtype=jnp.int32) // 2
    order = jnp.argsort(a_exp, stable=True)
    counts = jnp.bincount(a_exp, length=N_EXPERTS).astype(jnp.int32)
    padded = -(-counts // tm) * tm
    pad_end = jnp.cumsum(padded)
    pad_start = pad_end - padded
    start = jnp.cumsum(counts) - counts
    sorted_rank = jnp.argsort(order).astype(jnp.int32)
    pos = pad_start[a_exp] + sorted_rank - start[a_exp]
    tile_row = jnp.arange(EXPERT_TILES, dtype=jnp.int32) * tm
    tile_live = (tile_row < pad_end[-1]).astype(jnp.int32)
    tile_expert = jnp.minimum(jnp.searchsorted(pad_end, tile_row, side="right"), N_EXPERTS - 1).astype(jnp.int32)
    slot = jnp.arange(R_PAD, dtype=jnp.int32)
    e_s = tile_expert[slot // tm]
    off = slot - pad_start[e_s]
    real = (off < counts[e_s]) & (slot < pad_end[-1])
    a_s = order[jnp.clip(start[e_s] + off, 0, N_ASSIGN - 1)]
    src = jnp.where(real, a_tok[a_s], 0).astype(jnp.int32)
    roww = jnp.where(real, a_w[a_s], 0.0)
    chunk_real = jnp.sum(real.reshape(R_PAD // GATHER_ROWS, GATHER_ROWS), axis=1).astype(jnp.int32)
    last_live = jnp.maximum(pad_end[-1] // tm - 1, 0)
    tile_expert = jnp.where(tile_live > 0, tile_expert, tile_expert[last_live])
    pos2 = pos.reshape(NT, 2)
    return src, chunk_real, roww.reshape(R_PAD, 1), tile_expert, tile_live, pos2[:, 0], pos2[:, 1]


def _sample_rows(sample):
    return jnp.pad(sample, ((0, 0), (0, SROW - DEC_SEQ), (0, 0))).reshape(NS, sample.shape[-1])


def _rows_of(prompt, sample):
    return jnp.concatenate([prompt.reshape(NP, prompt.shape[-1]), _sample_rows(sample)], axis=0)


def _kv_rows(z_p, z_s, col):
    prompt = z_p[:, col:col + KV_ROW].reshape(1, BATCH, SEQ, NSA_KV_HEADS, 2, HEAD_DIM)
    sample = z_s[:, col:col + KV_ROW].reshape(DEC_BATCH, SROW, NSA_KV_HEADS, 2, HEAD_DIM)[None, :, :DEC_SEQ]
    return prompt, sample


def kernel(x_prompt, x_sample, cache_cmp, cache_sel, cache_win, state_hgrn, page_table, p_prompt, p_sample, g_attn, w_in, lb_logits, g_lin_out, g_q, g_kc, g_ks, g_kw, cmp_pos_logits, cmp_w1, cmp_w2, g_nsa_out, w_out, g_ffn, w_group_router, b_group_router, w_expert_router, b_expert_router, w_exp_gate, w_exp_up, w_exp_down, g_ple_in, w_ple_gate, w_ple_proj, g_ple_out):
    l = 0
    x_p = x_prompt.reshape(NP, D_MODEL)
    x_s = _sample_rows(x_sample)
    p_rows = _rows_of(p_prompt[l], p_sample[l])

    a_p = _rms_cast(x_p, g_attn[l], 512, BF16)
    a_s = _rms_cast(x_s, g_attn[l], NS, F32)
    tn = 512
    ones = jnp.ones((HEAD_DIM,), F32)
    kv_gain = lambda gk: jnp.tile(jnp.concatenate([gk, ones]), NSA_KV_HEADS)
    kv_sel = jnp.tile(jnp.concatenate([ones, 0 * ones]), NSA_KV_HEADS)
    gain = jnp.concatenate([jnp.ones((COL_NQ,), F32), jnp.tile(g_q[l], NSA_HEADS), jnp.ones((KV_ROW,), F32),
                            kv_gain(g_ks[l]), kv_gain(g_kw[l])]).reshape(1, N_MAIN)
    normsel = jnp.concatenate([jnp.zeros((COL_NQ,), F32), jnp.ones((NSA_WIDTH,), F32), jnp.zeros((KV_ROW,), F32),
                               kv_sel, kv_sel]).reshape(1, N_MAIN)
    tile_norm = np.zeros((N_MAIN // tn,), np.int32)
    tile_norm[COL_NQ // tn:COL_KVC // tn] = 1
    tile_norm[COL_KVS // tn:] = 1
    flags = jnp.asarray(tile_norm)
    z_p = _matmul(a_p, w_in[l], N_MAIN, 1024, tn, "groupnorm", (gain, normsel), ("row", "row"), flags=flags,
                  name="in_proj")
    z_s = _matmul(a_s, w_in[l], N_MAIN, NS, tn, "groupnorm", (gain, normsel), ("row", "row"), flags=flags,
                  name="in_proj")
    w_gate = jnp.pad(w_in[l][:, COL_GATE:COL_GATE + N_GATE], ((0, 0), (0, 128 - N_GATE)))
    gates_p = _matmul(a_p, w_gate, 128, 1024, 128, "sigmoid", name="gate_proj")
    gates_s = _matmul(a_s, w_gate, 128, NS, 128, "sigmoid", name="gate_proj")

    lb = jnp.cumsum(jax.nn.softmax(lb_logits.astype(F32), axis=0), axis=0)[l]
    ylin_p, st_p = _hgrn(z_p, lb, g_lin_out[l], None, batch=BATCH, rows=SEQ, c=128, t_valid=128, hp=4,
                         precise=False)
    ylin_s, st_s = _hgrn(z_s, lb, g_lin_out[l], state_hgrn[l], batch=DEC_BATCH, rows=SROW, c=SROW,
                         t_valid=DEC_SEQ, hp=LIN_HEADS, precise=True)

    cmp_p, cmp_s = _kv_rows(z_p, z_s, COL_KVC)
    sel_p, sel_s = _kv_rows(z_p, z_s, COL_KVS)
    win_p, win_new = _kv_rows(z_p, z_s, COL_KVW)
    logits_x = jnp.broadcast_to(cmp_pos_logits[l].transpose(2, 0, 1).reshape(CMP_LEN, KV_SUB, 1),
                                (CMP_LEN, KV_SUB, HEAD_DIM))
    pt_flat = page_table.reshape(-1)
    ckv_p = _compress_prompt(cmp_p.reshape(BATCH, SEQ, KV_SUB, HEAD_DIM), logits_x, cmp_w1[l], cmp_w2[l], g_kc[l])
    new_cmp = z_s[:, COL_KVC:COL_KVC + KV_ROW].reshape(DEC_BATCH, SROW, KV_SUB, HEAD_DIM)
    ckv_s = _compress_sample(cache_cmp[l].reshape(-1, PAGE_SIZE, KV_SUB, HEAD_DIM), pt_flat, new_cmp, logits_x,
                             cmp_w1[l], cmp_w2[l], g_kc[l])
    ocmp_p, mask_p = _cmp_select(z_p, ckv_p.reshape(BATCH, -1, KV_ROW), batch=BATCH, rows=SEQ, tq=256,
                                 nb=SEQ // SEL_BLOCK, pos_base=0, want_idx=False, precise=False)
    nb_s = -(-(PAST_LEN + DEC_SEQ) // SEL_BLOCK)
    ocmp_s, _, idx_s = _cmp_select(z_s, ckv_s.reshape(DEC_BATCH, -1, KV_ROW), batch=DEC_BATCH, rows=SROW,
                                   tq=SROW, nb=nb_s, pos_base=PAST_LEN, want_idx=True, precise=True)
    osel_p = _flash_prompt(z_p, "sel", mask_p)
    owin_p = _flash_prompt(z_p, "win")
    idx = idx_s.reshape(DEC_BATCH, SROW, NSA_KV_HEADS, HEAD_DIM)[:, :DEC_SEQ, :, :N_SEL].astype(jnp.int32)
    blk_no = jnp.concatenate([idx[..., :2], idx[..., 3:]], axis=-1)
    page = jnp.take_along_axis(page_table[:, None, None, :], blk_no // 2, axis=-1)
    phys_half = page * 2 + blk_no % 2
    osel_s = _sel_sample(z_s, cache_sel[l], phys_half.reshape(-1), blk_no.reshape(-1))
    owin_s = _win_sample(z_s, cache_win[l].reshape(DEC_BATCH, -1, HEAD_DIM))

    y_p = _mix(ylin_p, ocmp_p, osel_p, owin_p, gates_p, g_nsa_out[l], 256, BF16)
    y_s = _mix(ylin_s, ocmp_s, osel_s, owin_s, gates_s, g_nsa_out[l], 256, F32)
    attn_p = _matmul(y_p, w_out[l], D_MODEL, 1024, 512, "none", name="out_proj")
    attn_s = _matmul(y_s, w_out[l], D_MODEL, NS, 512, "none", name="out_proj")

    w_route = jnp.pad(jnp.concatenate([w_group_router[l], w_expert_router[l]], axis=1),
                      ((0, 0), (0, 128 - N_GROUPS - N_EXPERTS)))
    b_route = jnp.pad(jnp.concatenate([b_group_router[l], b_expert_router[l]]),
                      (0, 128 - N_GROUPS - N_EXPERTS)).reshape(1, 128)
    h1, m, route = _router(x_p, x_s, attn_p, attn_s, g_ffn[l], w_route, b_route, 128)
    src, chunk_real, roww, tile_expert, tile_live, pos0, pos1 = _dispatch_plan(route)
    xs = _row_gather(m, src, chunk_real, GATHER_ROWS, BF16)
    ys = _experts(xs, roww, tile_expert, tile_live, w_exp_gate[l], w_exp_up[l], w_exp_down[l])
    h2, n2 = _combine(ys, pos0, pos1, h1, g_ple_in[l], 128)

    pp = _matmul(p_rows.astype(BF16), w_ple_proj[l], D_MODEL, 256, D_MODEL, "rowrms",
                 (g_ple_out[l].reshape(1, D_MODEL),), ("row",), name="ple_proj")
    out_p = _matmul(n2, w_ple_gate[l], D_MODEL, 1024, 512, "ple", (h2, pp), ("full", "full"), name="ple_gate",
                    row0=0, m=NP)
    out_s = _matmul(n2, w_ple_gate[l], D_MODEL, NS, 512, "ple", (h2, pp), ("full", "full"), name="ple_gate",
                    row0=NP, m=NS)

    y_prompt = out_p.reshape(BATCH, SEQ, D_MODEL)
    y_sample = out_s.reshape(DEC_BATCH, SROW, D_MODEL)[:, :DEC_SEQ]
    w_keep = cache_win.shape[2]
    win_prompt = win_p[:, :, SEQ - min(WINDOW, SEQ):]
    win_sample = jnp.concatenate([cache_win[l], win_new[0]], axis=1)[None, :, DEC_SEQ:DEC_SEQ + w_keep]
    return (y_prompt, y_sample, cmp_p, cmp_s, sel_p, sel_s, win_prompt, win_sample,
            st_p[None], st_s[None])
```

```python
import functools
import math

import numpy as np
import jax
import jax.numpy as jnp
from jax import lax
from jax.experimental import pallas as pl
from jax.experimental.pallas import tpu as pltpu

F32 = jnp.float32
BF16 = jnp.bfloat16

D_MODEL = 4096
BATCH = 4
SEQ = 2048
DEC_BATCH = 32
DEC_SEQ = 4
PAST_LEN = 8192
PAGE_SIZE = 128
N_PAGES = PAST_LEN // PAGE_SIZE
LIN_WIDTH = 2048
NSA_WIDTH = 2048
HEAD_DIM = 128
LIN_HEADS = 16
NSA_HEADS = 16
NSA_KV_HEADS = 4
NSA_REP = 4
KV_WIDTH = NSA_KV_HEADS * HEAD_DIM
KV_ROW = 2 * KV_WIDTH
KV_SUB = 2 * NSA_KV_HEADS
CMP_STRIDE = 16
CMP_LEN = 32
CMP_HIDDEN = 256
SEL_BLOCK = 64
N_SEL = 16
WINDOW = 512
N_EXPERTS = 16
N_GROUPS = 4
EXP_PER_GROUP = 4
D_EXPERT = 1024
PLE_DIM = 256
EPS = 1e-6
NEG = -1e30
FORCE = 1e4
SCALE = HEAD_DIM ** -0.5

COL_LQ, COL_LF, COL_LI, COL_LG = 0, 2048, 4096, 6144
COL_NQ = 8192
COL_KVC, COL_KVS, COL_KVW = 10240, 11264, 12288
COL_GATE = 13312
N_MAIN = 13312
N_GATE = 3 * NSA_HEADS

NP = BATCH * SEQ
SROW = 16
NS = DEC_BATCH * SROW
NT = NP + NS

VMEM_LIMIT = 56 * 1024 * 1024

SLOPES = [2.0 ** (-8.0 * (h + 1) / NSA_HEADS) for h in range(NSA_HEADS)]


def _cparams(sem, vmem_limit=VMEM_LIMIT):
    return pltpu.CompilerParams(dimension_semantics=sem, vmem_limit_bytes=vmem_limit)


def _dot(a, b):
    return jnp.dot(a, b, preferred_element_type=F32)


def _dot_nt(a, b):
    return lax.dot_general(a, b, (((1,), (1,)), ((), ())), preferred_element_type=F32)


def _dot_tn(a, b):
    return lax.dot_general(a, b, (((0,), (0,)), ((), ())), preferred_element_type=F32)


def _split2(x):
    hi = x.astype(BF16)
    return hi, (x - hi.astype(F32)).astype(BF16)


def _dot3(a, b, form):
    a_hi, a_lo = _split2(a)
    b_hi, b_lo = _split2(b)
    if form == "tn":
        m = a.shape[1]
        two = _dot_tn(jnp.concatenate([a_hi, a_lo], axis=1), b_hi)
        return two[:m] + two[m:] + _dot_tn(a_hi, b_lo)
    f = _dot_nt if form == "nt" else _dot
    m = a.shape[0]
    two = f(jnp.concatenate([a_hi, a_lo], axis=0), b_hi)
    return two[:m] + two[m:] + f(a_hi, b_lo)


class _Mxu:
    def __init__(self, precise):
        self.precise = precise

    def op(self, x):
        return x.astype(F32) if self.precise else x.astype(BF16)

    def dot(self, a, b):
        return _dot3(a, b, "nn") if self.precise else _dot(a, b)

    def dot_nt(self, a, b):
        return _dot3(a, b, "nt") if self.precise else _dot_nt(a, b)

    def dot_tn(self, a, b):
        return _dot3(a, b, "tn") if self.precise else _dot_tn(a, b)


def _split3(x):
    a = x.astype(BF16)
    r = x - a.astype(F32)
    b = r.astype(BF16)
    c = (r - b.astype(F32)).astype(BF16)
    return a, b, c


def _rms_rows(x, g):
    ms = jnp.mean(x * x, axis=-1, keepdims=True)
    return x * lax.rsqrt(ms + EPS) * g


def _silu(x):
    return x * jax.nn.sigmoid(x)


def _two_group_specs(tm, width):
    npt = NP // tm
    return [pl.BlockSpec((tm, width), lambda i: (jnp.minimum(i, npt - 1), 0)),
            pl.BlockSpec((tm, width), lambda i: (jnp.maximum(i - npt, 0), 0))]


def _pick_group(p_ref, s_ref, tm):
    return jnp.where(pl.program_id(0) < NP // tm, p_ref[...], s_ref[...])


def _rms_cast_kernel(x_ref, g_ref, o_ref):
    o_ref[...] = _rms_rows(x_ref[...], g_ref[...]).astype(o_ref.dtype)


def _rms_cast(x, g, tm, dtype):
    n, d = x.shape
    return pl.pallas_call(
        _rms_cast_kernel,
        grid=(n // tm,),
        in_specs=[pl.BlockSpec((tm, d), lambda i: (i, 0)), pl.BlockSpec((1, d), lambda i: (0, 0))],
        out_specs=pl.BlockSpec((tm, d), lambda i: (i, 0)),
        out_shape=jax.ShapeDtypeStruct((n, d), dtype),
        compiler_params=_cparams(("arbitrary",)),
        name="rms_cast",
    )(x, g.reshape(1, d))


def _mm_kernel(flag_ref, x_ref, w_ref, *rest, epilogue, n_extra, precise):
    extras = rest[:n_extra]
    o_ref = rest[n_extra]
    wb_ref = rest[n_extra + 1]
    j = pl.program_id(0)

    @pl.when(pl.program_id(1) == 0)
    def _():
        w = w_ref[...]
        hi = w.astype(BF16)
        wb_ref[0] = hi
        if precise:
            wb_ref[1] = (w - hi.astype(F32)).astype(BF16)

    if precise:
        x = x_ref[...]
        x_hi = x.astype(BF16)
        x_lo = (x - x_hi.astype(F32)).astype(BF16)
        acc = _dot(x_hi, wb_ref[0]) + (_dot(x_hi, wb_ref[1]) + _dot(x_lo, wb_ref[0]))
    else:
        acc = _dot(x_ref[...], wb_ref[0])
    if epilogue == "groupnorm":
        gain_ref, sel_ref = extras
        normed = flag_ref[j] > 0

        @pl.when(normed)
        def _():
            for c in range(acc.shape[1] // HEAD_DIM):
                cs = slice(c * HEAD_DIM, (c + 1) * HEAD_DIM)
                blk = acc[:, cs]
                o_ref[:, cs] = jnp.where(sel_ref[:, cs] > 0, _rms_rows(blk, gain_ref[:, cs]), blk)

        @pl.when(jnp.logical_not(normed))
        def _():
            o_ref[...] = acc
    elif epilogue == "sigmoid":
        o_ref[...] = jax.nn.sigmoid(acc)
    elif epilogue == "none":
        o_ref[...] = acc
    elif epilogue == "rowrms":
        o_ref[...] = _rms_rows(acc, extras[0][...])
    elif epilogue == "ple":
        h_ref, pp_ref = extras
        o_ref[...] = h_ref[...] + jax.nn.sigmoid(acc) * pp_ref[...]
    else:
        raise ValueError(epilogue)


def _matmul(x, w, n_out, tm, tn, epilogue, extras=(), extra_kinds=(), flags=None, name="mm", row0=0, m=None):
    precise = x.dtype == F32
    k = x.shape[1]
    m = x.shape[0] if m is None else m
    blk0 = row0 // tm
    n_tiles = n_out // tn
    if flags is None:
        flags = jnp.zeros((n_tiles,), jnp.int32)
    specs = [pl.BlockSpec((tm, k), lambda j, i, f: (blk0 + i, 0)), pl.BlockSpec((k, tn), lambda j, i, f: (0, j))]
    for kind in extra_kinds:
        if kind == "row":
            specs.append(pl.BlockSpec((1, tn), lambda j, i, f: (0, j)))
        else:
            specs.append(pl.BlockSpec((tm, tn), lambda j, i, f: (blk0 + i, j)))
    return pl.pallas_call(
        functools.partial(_mm_kernel, epilogue=epilogue, n_extra=len(extras), precise=precise),
        grid_spec=pltpu.PrefetchScalarGridSpec(
            num_scalar_prefetch=1,
            grid=(n_tiles, m // tm),
            in_specs=specs,
            out_specs=pl.BlockSpec((tm, tn), lambda j, i, f: (i, j)),
            scratch_shapes=[pltpu.VMEM((2 if precise else 1, k, tn), BF16)],
        ),
        out_shape=jax.ShapeDtypeStruct((m, n_out), F32),
        compiler_params=_cparams(("arbitrary", "arbitrary")),
        name=name,
    )(flags, x, w, *extras)


def _hgrn_masks(c):
    levels = int(math.log2(c))
    sums = np.zeros((levels + 2, c, c), np.float32)
    pair = np.zeros((levels + 1, c, c), np.float32)
    pair[0] = np.eye(c)
    for lv in range(levels):
        h = 1 << lv
        for t in range(c):
            base = (t // (2 * h)) * 2 * h
            m = base + h - 1
            if t % (2 * h) >= h:
                sums[lv, t, m + 1:t + 1] = 1.0
                pair[lv + 1, t, base:base + h] = 1.0
            else:
                sums[lv, t, t + 1:m + 1] = 1.0
    for t in range(c):
        sums[levels, t, :t + 1] = 1.0
        sums[levels + 1, t, t + 1:] = 1.0
    return sums.reshape((levels + 2) * c, c), pair


def _hgrn_kernel(q_ref, f_ref, v_ref, g_ref, lb_ref, gout_ref, sums_ref, pair_ref, *rest, c, n_chunks, t_valid,
                 hp, has_s0, precise):
    mx = _Mxu(precise)
    y_ref, s_ref, st_scr = rest[-3:]
    for hh in range(hp):
        st_scr[hh] = rest[0][hh].T if has_s0 else jnp.zeros((HEAD_DIM, HEAD_DIM), F32)
    levels = int(math.log2(c))
    row = lax.broadcasted_iota(jnp.int32, (c, 1), 0)

    def chunk(ci, carry):
        r0 = pl.multiple_of(ci * c, c)
        for hh in range(hp):
            hs = slice(hh * HEAD_DIM, (hh + 1) * HEAD_DIM)
            lb = lb_ref[hh]
            q = q_ref[pl.ds(r0, c), hs]
            z = f_ref[pl.ds(r0, c), hs]
            v = v_ref[pl.ds(r0, c), hs]
            logf = jnp.log(lb + (1.0 - lb) * jax.nn.sigmoid(z))
            k = (1.0 - lb) * jax.nn.sigmoid(-z)
            if t_valid < c:
                live = row < t_valid
                logf = jnp.where(live, logf, 0.0)
                k = jnp.where(live, k, 0.0)
            parts = jnp.concatenate(_split3(logf), axis=1)
            r = _dot(sums_ref[...], parts)
            ex = jnp.exp(r[:, :HEAD_DIM] + r[:, HEAD_DIM:2 * HEAD_DIM] + r[:, 2 * HEAD_DIM:])
            attn = pair_ref[0] * mx.dot_nt(mx.op(q), mx.op(k))
            for lv in range(levels):
                w = ex[lv * c:(lv + 1) * c]
                attn = attn + pair_ref[lv + 1] * mx.dot_nt(mx.op(q * w), mx.op(k * w))
            e_cum = ex[levels * c:(levels + 1) * c]
            e_rev = ex[(levels + 1) * c:(levels + 2) * c]
            st = st_scr[hh]
            vb = mx.op(v)
            o = mx.dot_nt(mx.op(q * e_cum), mx.op(st)) + mx.dot(mx.op(attn), vb)
            st_scr[hh] = st * e_cum[c - 1:c, :] + mx.dot_tn(vb, mx.op(k * e_rev))
            y_ref[pl.ds(r0, c), hs] = _rms_rows(o, gout_ref[...]) * _silu(g_ref[pl.ds(r0, c), hs])
        return carry

    lax.fori_loop(0, n_chunks, chunk, 0)
    for hh in range(hp):
        s_ref[hh] = st_scr[hh].T


def _hgrn(z, lb, g_lin_out, s0, *, batch, rows, c, t_valid, hp, precise):
    sums, pair = _hgrn_masks(c)
    has_s0 = s0 is not None
    w = hp * HEAD_DIM

    def zspec(col):
        cb = col // w
        return pl.BlockSpec((rows, w), lambda b, h: (b, cb + h))

    const2 = lambda b, h: (0, 0)
    st_spec = pl.BlockSpec((None, hp, HEAD_DIM, HEAD_DIM), lambda b, h: (b, h, 0, 0))
    in_specs = [zspec(COL_LQ), zspec(COL_LF), zspec(COL_LI), zspec(COL_LG),
                pl.BlockSpec((hp, 1, HEAD_DIM), lambda b, h: (h, 0, 0)),
                pl.BlockSpec((1, HEAD_DIM), const2),
                pl.BlockSpec(sums.shape, const2),
                pl.BlockSpec(pair.shape, lambda b, h: (0, 0, 0))]
    args = [z, z, z, z, lb.reshape(LIN_HEADS, 1, HEAD_DIM), g_lin_out.reshape(1, HEAD_DIM),
            jnp.asarray(sums, BF16), jnp.asarray(pair, F32)]
    if has_s0:
        in_specs.append(st_spec)
        args.append(s0)
    return pl.pallas_call(
        functools.partial(_hgrn_kernel, c=c, n_chunks=rows // c, t_valid=t_valid, hp=hp, has_s0=has_s0,
                          precise=precise),
        grid=(batch, LIN_HEADS // hp),
        in_specs=in_specs,
        out_specs=[pl.BlockSpec((rows, w), lambda b, h: (b, h)), st_spec],
        out_shape=[jax.ShapeDtypeStruct((batch * rows, LIN_WIDTH), F32),
                   jax.ShapeDtypeStruct((batch, LIN_HEADS, HEAD_DIM, HEAD_DIM), F32)],
        scratch_shapes=[pltpu.VMEM((hp, HEAD_DIM, HEAD_DIM), F32)],
        compiler_params=_cparams(("arbitrary", "arbitrary")),
        name="hgrn2",
    )(*args)


def _pool_weights(logit_ref):
    lg = logit_ref[...]
    e = jnp.exp(lg - jnp.max(lg, axis=0, keepdims=True))
    a = e / jnp.sum(e, axis=0, keepdims=True)
    return a[:CMP_STRIDE], a[CMP_STRIDE:]


def _pool(rows, a1, a2):
    n = rows.shape[0] // CMP_STRIDE
    r = rows.reshape(n, CMP_STRIDE, KV_SUB, HEAD_DIM)
    return jnp.sum(r * a1[None], axis=1), jnp.sum(r * a2[None], axis=1)


def _cmp_mlp(pooled, w1_ref, w2_ref, gkc_ref, precise):
    mx = _Mxu(precise)
    r = pooled.shape[0]
    xs = pooled.reshape(r * KV_SUB, HEAD_DIM)
    xb = mx.op(xs)
    outs = []
    for t in range(2):
        hid = _silu(mx.dot(xb, mx.op(w1_ref[t])))
        outs.append(xs + mx.dot(mx.op(hid), mx.op(w2_ref[t])))
    is_key = (lax.broadcasted_iota(jnp.int32, (r * KV_SUB, 1), 0) & 1) == 0
    out = jnp.where(is_key, _rms_rows(outs[0], gkc_ref[...]), outs[1])
    return out.reshape(r, KV_SUB, HEAD_DIM)


def _compress_prompt_kernel(kv_ref, logit_ref, w1_ref, w2_ref, gkc_ref, o_ref):
    a1, a2 = _pool_weights(logit_ref)
    p1, p2 = _pool(kv_ref[...], a1, a2)
    pooled = p1 + jnp.concatenate([p2[1:], p2[:1]], axis=0)
    o_ref[...] = _cmp_mlp(pooled, w1_ref, w2_ref, gkc_ref, precise=False)


def _compress_prompt(kv, logits_x, w1, w2, g_kc):
    nblk = SEQ // CMP_STRIDE
    cst = lambda b: (0, 0)
    cst3 = lambda b: (0, 0, 0)
    return pl.pallas_call(
        _compress_prompt_kernel,
        grid=(BATCH,),
        in_specs=[pl.BlockSpec((None, SEQ, KV_SUB, HEAD_DIM), lambda b: (b, 0, 0, 0)),
                  pl.BlockSpec((CMP_LEN, KV_SUB, HEAD_DIM), cst3),
                  pl.BlockSpec(w1.shape, cst3), pl.BlockSpec(w2.shape, cst3),
                  pl.BlockSpec((1, HEAD_DIM), cst)],
        out_specs=pl.BlockSpec((None, nblk, KV_SUB, HEAD_DIM), lambda b: (b, 0, 0, 0)),
        out_shape=jax.ShapeDtypeStruct((BATCH, nblk, KV_SUB, HEAD_DIM), F32),
        compiler_params=_cparams(("arbitrary",)),
        name="compress_prompt",
    )(kv, logits_x, w1, w2, g_kc.reshape(1, HEAD_DIM))


PAGES_PER_STEP = 16


def _compress_sample_kernel(pt_ref, *refs):
    pages = refs[:PAGES_PER_STEP]
    nxt_ref, new_ref, logit_ref, w1_ref, w2_ref, gkc_ref, o_ref = refs[PAGES_PER_STEP:]
    a1, a2 = _pool_weights(logit_ref)
    last = pl.program_id(1) == pl.num_programs(1) - 1
    nxt = jnp.where(last, new_ref[...], nxt_ref[...])
    tail = jnp.sum(nxt * a2, axis=0, keepdims=True)
    pooled = [None] * PAGES_PER_STEP
    for kk in reversed(range(PAGES_PER_STEP)):
        p1, p2 = _pool(pages[kk][...], a1, a2)
        pooled[kk] = p1 + jnp.concatenate([p2[1:], tail], axis=0)
        tail = p2[:1]
    o_ref[...] = _cmp_mlp(jnp.concatenate(pooled, axis=0), w1_ref, w2_ref, gkc_ref, precise=True)


def _compress_sample(cache, pt_flat, new_rows, logits_x, w1, w2, g_kc):
    steps = N_PAGES // PAGES_PER_STEP
    rows = PAGES_PER_STEP * PAGE_SIZE // CMP_STRIDE
    cst = lambda b, s, pt: (0, 0)
    cst3 = lambda b, s, pt: (0, 0, 0)

    def page_spec(kk):
        return pl.BlockSpec((None, PAGE_SIZE, KV_SUB, HEAD_DIM),
                            lambda b, s, pt: (pt[b * N_PAGES + s * PAGES_PER_STEP + kk], 0, 0, 0))

    nxt_spec = pl.BlockSpec(
        (None, CMP_STRIDE, KV_SUB, HEAD_DIM),
        lambda b, s, pt: (pt[b * N_PAGES + jnp.minimum((s + 1) * PAGES_PER_STEP, N_PAGES - 1)], 0, 0, 0))
    new_spec = pl.BlockSpec((None, CMP_STRIDE, KV_SUB, HEAD_DIM), lambda b, s, pt: (b, 0, 0, 0))
    nc = PAST_LEN // CMP_STRIDE
    return pl.pallas_call(
        _compress_sample_kernel,
        grid_spec=pltpu.PrefetchScalarGridSpec(
            num_scalar_prefetch=1,
            grid=(DEC_BATCH, steps),
            in_specs=[page_spec(kk) for kk in range(PAGES_PER_STEP)] + [
                nxt_spec, new_spec, pl.BlockSpec((CMP_LEN, KV_SUB, HEAD_DIM), cst3),
                pl.BlockSpec(w1.shape, cst3), pl.BlockSpec(w2.shape, cst3), pl.BlockSpec((1, HEAD_DIM), cst)],
            out_specs=pl.BlockSpec((None, rows, KV_SUB, HEAD_DIM), lambda b, s, pt: (b, s, 0, 0)),
        ),
        out_shape=jax.ShapeDtypeStruct((DEC_BATCH, nc, KV_SUB, HEAD_DIM), F32),
        compiler_params=_cparams(("arbitrary", "arbitrary")),
        name="compress_sample",
    )(pt_flat, *([cache] * (PAGES_PER_STEP + 1)), new_rows, logits_x, w1, w2, g_kc.reshape(1, HEAD_DIM))


def _overlap_matrix(ncp, nbp, nb):
    c0 = np.arange(ncp)[:, None] * CMP_STRIDE
    b0 = np.arange(nbp)[None, :] * SEL_BLOCK
    ov = (c0 < b0 + SEL_BLOCK) & (c0 + CMP_LEN > b0) & (np.arange(nbp)[None, :] < nb)
    return ov.astype(np.float32)


def _cmp_select_kernel(q_ref, ckv_ref, ov_ref, o_ref, m_ref, *maybe_idx, tq, nb, pos_base, tiled, precise):
    mx = _Mxu(precise)
    ncp = ckv_ref.shape[0]
    nbp = ov_ref.shape[1]
    t0 = pos_base + (pl.program_id(1) * tq if tiled else 0)
    qp = t0 + lax.broadcasted_iota(jnp.int32, (tq, 1), 0)
    cend = lax.broadcasted_iota(jnp.int32, (1, ncp), 1) * CMP_STRIDE + (CMP_LEN - 1)
    valid = qp >= cend
    validf = valid.astype(F32)
    dist = (qp - cend).astype(F32)
    n_i = lax.broadcasted_iota(jnp.int32, (1, nbp), 1)
    qblk = lax.shift_right_arithmetic(qp, int(math.log2(SEL_BLOCK)))
    forced = (n_i == 0) | (n_i == qblk) | (n_i == qblk - 1)
    future = n_i * SEL_BLOCK > qp
    lane = lax.broadcasted_iota(jnp.int32, (1, HEAD_DIM), 1)
    for g in range(NSA_KV_HEADS):
        gs = slice(g * HEAD_DIM, (g + 1) * HEAD_DIM)
        kcg = mx.op(ckv_ref[:, 2 * g * HEAD_DIM:(2 * g + 1) * HEAD_DIM])
        vcg = mx.op(ckv_ref[:, (2 * g + 1) * HEAD_DIM:(2 * g + 2) * HEAD_DIM])
        q4 = mx.op(jnp.concatenate([q_ref[:, (g * NSA_REP + r) * HEAD_DIM:(g * NSA_REP + r + 1) * HEAD_DIM]
                                    for r in range(NSA_REP)], axis=0))
        s4 = mx.dot_nt(q4, kcg) * SCALE
        psum = jnp.zeros((tq, ncp), F32)
        for r in range(NSA_REP):
            h = g * NSA_REP + r
            s = jnp.where(valid, s4[r * tq:(r + 1) * tq] - SLOPES[h] * dist, NEG)
            e = jnp.exp(s - jnp.max(s, axis=-1, keepdims=True))
            p = e / jnp.sum(e, axis=-1, keepdims=True) * validf
            o_ref[:, h * HEAD_DIM:(h + 1) * HEAD_DIM] = mx.dot(mx.op(p), vcg)
            psum = psum + p
        ov = ov_ref[...]
        imp = sum(_dot(part, ov) for part in _split3(psum))
        score = jnp.where(forced, FORCE, jnp.where(future, -1.0, imp))
        score = jnp.where(n_i < nb, score, -2.0)
        rank = jnp.zeros((tq, nbp), F32)
        for m in range(nb):
            col = score[:, m:m + 1]
            ahead = (col > score) | ((col == score) & (n_i > m))
            rank = rank + jnp.where(ahead, 1.0, 0.0)
        chosen = (rank < float(N_SEL)) & (n_i < nb) & jnp.logical_not(future)
        m_ref[:, g * nbp:(g + 1) * nbp] = chosen.astype(F32)
        if maybe_idx:
            nf = n_i.astype(F32)
            idx = jnp.zeros((tq, HEAD_DIM), F32)
            for jj in range(N_SEL):
                pick = jnp.sum(jnp.where(rank == float(jj), nf, 0.0), axis=-1, keepdims=True)
                idx = jnp.where(lane == jj, pick, idx)
            maybe_idx[0][:, gs] = idx


def _cmp_select(z, ckv, *, batch, rows, tq, nb, pos_base, want_idx, precise):
    ncp = ckv.shape[1]
    nbp = -(-nb // 128) * 128
    ov = jnp.asarray(_overlap_matrix(ncp, nbp, nb), BF16)
    n_q = rows // tq
    out_specs = [pl.BlockSpec((tq, NSA_WIDTH), lambda b, i: (b * n_q + i, 0)),
                 pl.BlockSpec((tq, NSA_KV_HEADS * nbp), lambda b, i: (b * n_q + i, 0))]
    out_shape = [jax.ShapeDtypeStruct((batch * rows, NSA_WIDTH), F32),
                 jax.ShapeDtypeStruct((batch * rows, NSA_KV_HEADS * nbp), F32)]
    if want_idx:
        out_specs.append(pl.BlockSpec((tq, NSA_KV_HEADS * HEAD_DIM), lambda b, i: (b * n_q + i, 0)))
        out_shape.append(jax.ShapeDtypeStruct((batch * rows, NSA_KV_HEADS * HEAD_DIM), F32))
    return pl.pallas_call(
        functools.partial(_cmp_select_kernel, tq=tq, nb=nb, pos_base=pos_base, tiled=n_q > 1, precise=precise),
        grid=(batch, n_q),
        in_specs=[pl.BlockSpec((tq, NSA_WIDTH), lambda b, i: (b * n_q + i, COL_NQ // NSA_WIDTH)),
                  pl.BlockSpec((None, ncp, KV_ROW), lambda b, i: (b, 0, 0)),
                  pl.BlockSpec(ov.shape, lambda b, i: (0, 0))],
        out_specs=out_specs,
        out_shape=out_shape,
        compiler_params=_cparams(("arbitrary", "arbitrary")),
        name="cmp_select",
    )(z, ckv, ov)


FLASH_TQ = 512
FLASH_TK = {"sel": 1024, "win": 512}


def _flash_kernel(slope_ref, q_ref, kv_ref, *rest, mode):
    if mode == "sel":
        mask_ref, exp_ref, o_ref, m_scr, l_scr, acc_scr = rest
    else:
        o_ref, m_scr, l_scr, acc_scr = rest
    tq, tk = FLASH_TQ, FLASH_TK[mode]
    g = pl.program_id(1)
    qi = pl.program_id(2)
    kj = pl.program_id(3)
    if mode == "sel":
        ka = kj
        needed = kj * tk <= qi * tq + (tq - 1)
    else:
        ka = qi * (tq // tk) - WINDOW // tk + kj
        needed = ka >= 0

    @pl.when(kj == 0)
    def _():
        m_scr[...] = jnp.full_like(m_scr, NEG)
        l_scr[...] = jnp.zeros_like(l_scr)
        acc_scr[...] = jnp.zeros_like(acc_scr)

    @pl.when(needed)
    def _():
        q4 = jnp.concatenate([q_ref[:, r * HEAD_DIM:(r + 1) * HEAD_DIM] for r in range(NSA_REP)],
                             axis=0).astype(BF16)
        kb = kv_ref[:, :HEAD_DIM].astype(BF16)
        vb = kv_ref[:, HEAD_DIM:].astype(BF16)
        s4 = _dot_nt(q4, kb) * SCALE
        qpos = qi * tq + lax.broadcasted_iota(jnp.int32, (tq, 1), 0)
        kpos = ka * tk + lax.broadcasted_iota(jnp.int32, (1, tk), 1)
        d = qpos - kpos
        ok = d >= 0
        if mode == "sel":
            ok = ok & (_dot(mask_ref[...].astype(BF16), exp_ref[...]) > 0.5)
        else:
            ok = ok & (d < WINDOW)
        df = d.astype(F32)
        for r in range(NSA_REP):
            s = jnp.where(ok, s4[r * tq:(r + 1) * tq] - slope_ref[g * NSA_REP + r] * df, NEG)
            m_prev = m_scr[r]
            m_new = jnp.maximum(m_prev, jnp.max(s, axis=-1, keepdims=True))
            p = jnp.where(ok, jnp.exp(s - m_new), 0.0)
            alpha = jnp.exp(m_prev - m_new)
            l_scr[r] = alpha * l_scr[r] + jnp.sum(p, axis=-1, keepdims=True)
            acc_scr[r] = alpha * acc_scr[r] + _dot(p.astype(BF16), vb)
            m_scr[r] = m_new

    @pl.when(kj == pl.num_programs(3) - 1)
    def _():
        for r in range(NSA_REP):
            o_ref[:, r * HEAD_DIM:(r + 1) * HEAD_DIM] = acc_scr[r] / l_scr[r]


def _flash_prompt(z, mode, mask=None):
    tq, tk = FLASH_TQ, FLASH_TK[mode]
    n_q = SEQ // tq
    slopes = jnp.asarray(SLOPES, F32)
    qw = NSA_REP * HEAD_DIM
    kvw = 2 * HEAD_DIM
    if mode == "sel":
        n_kv = SEQ // tk
        col0 = COL_KVS // kvw
        kv_map = lambda b, g, i, j: (b * n_kv + jnp.minimum(j, (i * tq + tq - 1) // tk), col0 + g)
    else:
        n_kv = WINDOW // tk + tq // tk
        col0 = COL_KVW // kvw
        kv_map = lambda b, g, i, j: (b * (SEQ // tk) + jnp.maximum(i * (tq // tk) - WINDOW // tk + j, 0), col0 + g)
    in_specs = [pl.BlockSpec(memory_space=pltpu.SMEM),
                pl.BlockSpec((tq, qw), lambda b, g, i, j: (b * n_q + i, COL_NQ // qw + g)),
                pl.BlockSpec((tk, kvw), kv_map)]
    args = [slopes, z, z]
    if mode == "sel":
        nbp = mask.shape[1] // NSA_KV_HEADS
        blk = (np.arange(SEQ)[None, :] // SEL_BLOCK) == np.arange(nbp)[:, None]
        expand = jnp.asarray(blk.reshape(nbp, n_kv, tk).transpose(1, 0, 2).astype(np.float32), BF16)
        in_specs += [pl.BlockSpec((tq, nbp), lambda b, g, i, j: (b * n_q + i, g)),
                     pl.BlockSpec((None, nbp, tk), lambda b, g, i, j: (jnp.minimum(j, (i * tq + tq - 1) // tk), 0, 0))]
        args += [mask, expand]
    return pl.pallas_call(
        functools.partial(_flash_kernel, mode=mode),
        grid=(BATCH, NSA_KV_HEADS, n_q, n_kv),
        in_specs=in_specs,
        out_specs=pl.BlockSpec((tq, qw), lambda b, g, i, j: (b * n_q + i, g)),
        out_shape=jax.ShapeDtypeStruct((NP, NSA_WIDTH), F32),
        scratch_shapes=[pltpu.VMEM((NSA_REP, tq, 1), F32), pltpu.VMEM((NSA_REP, tq, 1), F32),
                        pltpu.VMEM((NSA_REP, tq, HEAD_DIM), F32)],
        compiler_params=_cparams(("arbitrary",) * 4),
        name="flash_" + mode,
    )(*args)


def _softmax_parts(scores, oks):
    mx = functools.reduce(jnp.maximum, [jnp.max(jnp.where(ok, s, NEG), axis=-1, keepdims=True)
                                        for s, ok in zip(scores, oks)])
    es = [jnp.where(ok, jnp.exp(jnp.where(ok, s, NEG) - mx), 0.0) for s, ok in zip(scores, oks)]
    den = functools.reduce(lambda a, b: a + b, [jnp.sum(e, axis=-1, keepdims=True) for e in es])
    return [e / den for e in es]


def _win_sample_kernel(q_ref, cache_ref, new_ref, o_ref):
    mx = _Mxu(True)
    wk = cache_ref.shape[0] // KV_SUB
    row = lax.broadcasted_iota(jnp.int32, (NSA_REP * SROW, 1), 0)
    qi = row & (SROW - 1)
    jc = lax.broadcasted_iota(jnp.int32, (1, wk), 1)
    jn = lax.broadcasted_iota(jnp.int32, (1, SROW), 1)
    d_c = qi + (wk - jc)
    d_n = qi - jn
    ok_c = d_c < WINDOW
    ok_n = (d_n >= 0) & (jn < DEC_SEQ)
    for g in range(NSA_KV_HEADS):
        kcol = slice(g * 2 * HEAD_DIM, g * 2 * HEAD_DIM + HEAD_DIM)
        vcol = slice(g * 2 * HEAD_DIM + HEAD_DIM, (g + 1) * 2 * HEAD_DIM)
        k_c = cache_ref[pl.ds(2 * g, wk, stride=KV_SUB), :]
        v_c = cache_ref[pl.ds(2 * g + 1, wk, stride=KV_SUB), :]
        q4 = jnp.concatenate([q_ref[:, (g * NSA_REP + r) * HEAD_DIM:(g * NSA_REP + r + 1) * HEAD_DIM]
                              for r in range(NSA_REP)], axis=0)
        s_c = mx.dot_nt(q4, k_c) * SCALE
        s_n = mx.dot_nt(q4, new_ref[:, kcol]) * SCALE
        slope = jnp.concatenate([jnp.full((SROW, 1), SLOPES[g * NSA_REP + r], F32) for r in range(NSA_REP)], axis=0)
        s_c = s_c - slope * d_c.astype(F32)
        s_n = s_n - slope * d_n.astype(F32)
        p_c, p_n = _softmax_parts([s_c, s_n], [ok_c, ok_n])
        o = mx.dot(p_c, v_c) + mx.dot(p_n, new_ref[:, vcol])
        for r in range(NSA_REP):
            h = g * NSA_REP + r
            o_ref[:, h * HEAD_DIM:(h + 1) * HEAD_DIM] = o[r * SROW:(r + 1) * SROW]


def _win_sample(z, cache_win):
    wk8 = cache_win.shape[1]
    return pl.pallas_call(
        _win_sample_kernel,
        grid=(DEC_BATCH,),
        in_specs=[pl.BlockSpec((SROW, NSA_WIDTH), lambda b: (b, COL_NQ // NSA_WIDTH)),
                  pl.BlockSpec((None, wk8, HEAD_DIM), lambda b: (b, 0, 0)),
                  pl.BlockSpec((SROW, KV_ROW), lambda b: (b, COL_KVW // KV_ROW))],
        out_specs=pl.BlockSpec((SROW, NSA_WIDTH), lambda b: (b, 0)),
        out_shape=jax.ShapeDtypeStruct((NS, NSA_WIDTH), F32),
        compiler_params=_cparams(("arbitrary",)),
        name="win_sample",
    )(z, cache_win, z)


N_GATHER = N_SEL - 1


def _sel_sample_kernel(phys_ref, blk_ref, slope_ref, q_ref, new_ref, *rest):
    blocks = rest[:N_GATHER]
    o_ref = rest[-1]
    b = pl.program_id(0)
    g = pl.program_id(1)
    i = pl.program_id(2)

    @pl.when(i == 0)
    def _():
        o_ref[...] = jnp.zeros_like(o_ref)

    base = ((b * DEC_SEQ + i) * NSA_KV_HEADS + g) * N_GATHER
    rr = lax.broadcasted_iota(jnp.int32, (8, 1), 0)
    mine = rr == i
    mx = _Mxu(True)
    nq = 16
    q4 = jnp.concatenate([jnp.sum(jnp.where(mine, q_ref[0:8, r * HEAD_DIM:(r + 1) * HEAD_DIM], 0.0),
                                  axis=0, keepdims=True) for r in range(NSA_REP)]
                         + [jnp.zeros((nq - NSA_REP, HEAD_DIM), F32)], axis=0)
    rq = lax.broadcasted_iota(jnp.int32, (nq, 1), 0)
    slope = jnp.zeros((nq, 1), F32)
    for r in range(NSA_REP):
        slope = jnp.where(rq == r, slope_ref[g * NSA_REP + r], slope)
    qp = PAST_LEN + i
    lane2 = lax.broadcasted_iota(jnp.int32, (1, 2 * SEL_BLOCK), 1)
    off = lax.shift_right_logical(lane2, 1)
    rows = jnp.concatenate([blocks[jj][...].reshape(2 * SEL_BLOCK, HEAD_DIM) for jj in range(N_GATHER)], axis=0)
    d = jnp.concatenate([qp - (blk_ref[base + jj] * SEL_BLOCK + off) for jj in range(N_GATHER)], axis=1)
    is_key = jnp.concatenate([(lane2 & 1) == 0] * N_GATHER, axis=1)
    jn = lax.broadcasted_iota(jnp.int32, (1, SROW), 1)
    d_n = i - jn
    scores = [mx.dot_nt(q4, rows) * SCALE - slope * d.astype(F32),
              mx.dot_nt(q4, new_ref[:, :HEAD_DIM]) * SCALE - slope * d_n.astype(F32)]
    oks = [jnp.broadcast_to((d >= 0) & is_key, scores[0].shape),
           jnp.broadcast_to((d_n >= 0) & (jn < DEC_SEQ), scores[1].shape)]
    p_old, p_new = _softmax_parts(scores, oks)
    o = mx.dot(pltpu.roll(p_old, 1, axis=1), rows) + mx.dot(p_new, new_ref[:, HEAD_DIM:])
    for r in range(NSA_REP):
        cs = slice(r * HEAD_DIM, (r + 1) * HEAD_DIM)
        o_ref[0:8, cs] = jnp.where(mine, o[r:r + 1], o_ref[0:8, cs])


def _sel_sample(z, cache_sel, phys_half, blk_no):
    kvw = 2 * HEAD_DIM
    qw = NSA_REP * HEAD_DIM
    halves = cache_sel.reshape(-1, SEL_BLOCK, NSA_KV_HEADS, 2, HEAD_DIM)

    def blk_spec(jj):
        return pl.BlockSpec(
            (None, SEL_BLOCK, None, 2, HEAD_DIM),
            lambda b, g, i, ph, bn: (ph[((b * DEC_SEQ + i) * NSA_KV_HEADS + g) * N_GATHER + jj], 0, g, 0, 0))

    return pl.pallas_call(
        _sel_sample_kernel,
        grid_spec=pltpu.PrefetchScalarGridSpec(
            num_scalar_prefetch=2,
            grid=(DEC_BATCH, NSA_KV_HEADS, DEC_SEQ),
            in_specs=[pl.BlockSpec(memory_space=pltpu.SMEM),
                      pl.BlockSpec((SROW, qw), lambda b, g, i, ph, bn: (b, COL_NQ // qw + g)),
                      pl.BlockSpec((SROW, kvw), lambda b, g, i, ph, bn: (b, COL_KVS // kvw + g))]
            + [blk_spec(jj) for jj in range(N_GATHER)],
            out_specs=pl.BlockSpec((SROW, qw), lambda b, g, i, ph, bn: (b, g)),
        ),
        out_shape=jax.ShapeDtypeStruct((NS, NSA_WIDTH), F32),
        compiler_params=_cparams(("arbitrary",) * 3),
        name="sel_sample",
    )(phys_half, blk_no, jnp.asarray(SLOPES, F32), z, z, *([halves] * N_GATHER))


def _mix_kernel(ylin_ref, cmp_ref, sel_ref, win_ref, gate_ref, gn_ref, y_ref):
    gates = gate_ref[...]
    parts = []
    for h in range(NSA_HEADS):
        hs = slice(h * HEAD_DIM, (h + 1) * HEAD_DIM)
        parts.append(gates[:, h:h + 1] * cmp_ref[:, hs]
                     + gates[:, NSA_HEADS + h:NSA_HEADS + h + 1] * sel_ref[:, hs]
                     + gates[:, 2 * NSA_HEADS + h:2 * NSA_HEADS + h + 1] * win_ref[:, hs])
    o = jnp.concatenate(parts, axis=1)
    y_ref[:, :LIN_WIDTH] = ylin_ref[...].astype(y_ref.dtype)
    y_ref[:, LIN_WIDTH:] = _rms_rows(o, gn_ref[...]).astype(y_ref.dtype)


def _mix(ylin, o_cmp, o_sel, o_win, gates, g_nsa_out, tm, dtype):
    n = ylin.shape[0]
    row = lambda i: (i, 0)
    return pl.pallas_call(
        _mix_kernel,
        grid=(n // tm,),
        in_specs=[pl.BlockSpec((tm, LIN_WIDTH), row)] + [pl.BlockSpec((tm, NSA_WIDTH), row)] * 3
        + [pl.BlockSpec((tm, 128), row), pl.BlockSpec((1, NSA_WIDTH), lambda i: (0, 0))],
        out_specs=pl.BlockSpec((tm, D_MODEL), row),
        out_shape=jax.ShapeDtypeStruct((n, D_MODEL), dtype),
        compiler_params=_cparams(("arbitrary",)),
        name="mix",
    )(ylin, o_cmp, o_sel, o_win, gates, g_nsa_out.reshape(1, NSA_WIDTH))


def _router_kernel(xp_ref, xs_ref, ap_ref, as_ref, g_ref, wr_ref, br_ref, h_ref, m_ref, r_ref, *, tm):
    h = _pick_group(xp_ref, xs_ref, tm) + _pick_group(ap_ref, as_ref, tm)
    h_ref[...] = h
    m = _rms_rows(h, g_ref[...])
    m_ref[...] = m
    logits = _dot3(m, wr_ref[...], "nn") + br_ref[...]
    lane = lax.broadcasted_iota(jnp.int32, (1, 128), 1)
    big = 1024
    is_g = lane < N_GROUPS
    gl = jnp.where(is_g, logits, -jnp.inf)
    gmax = jnp.max(gl, axis=-1, keepdims=True)
    gsel = jnp.min(jnp.where(gl == gmax, lane, big), axis=-1, keepdims=True)
    gw = 1.0 / jnp.sum(jnp.where(is_g, jnp.exp(gl - gmax), 0.0), axis=-1, keepdims=True)
    e_id = lane - N_GROUPS
    in_grp = (e_id >= 0) & (e_id < N_EXPERTS) & (lax.shift_right_arithmetic(e_id, int(math.log2(EXP_PER_GROUP))) == gsel)
    el = jnp.where(in_grp, logits, -jnp.inf)
    v1 = jnp.max(el, axis=-1, keepdims=True)
    i1 = jnp.min(jnp.where(el == v1, lane, big), axis=-1, keepdims=True)
    el2 = jnp.where(lane == i1, -jnp.inf, el)
    v2 = jnp.max(el2, axis=-1, keepdims=True)
    i2 = jnp.min(jnp.where(el2 == v2, lane, big), axis=-1, keepdims=True)
    t = jnp.exp(v2 - v1)
    w1 = gw / (1.0 + t)
    w2 = gw * t / (1.0 + t)
    out = jnp.where(lane == 0, (i1 - N_GROUPS).astype(F32), 0.0)
    out = jnp.where(lane == 1, (i2 - N_GROUPS).astype(F32), out)
    out = jnp.where(lane == 2, w1, out)
    out = jnp.where(lane == 3, w2, out)
    r_ref[...] = out


def _router(x_p, x_s, attn_p, attn_s, g_ffn, w_route, b_route, tm):
    row = lambda i: (i, 0)
    cst = lambda i: (0, 0)
    return pl.pallas_call(
        functools.partial(_router_kernel, tm=tm),
        grid=(NT // tm,),
        in_specs=_two_group_specs(tm, D_MODEL) + _two_group_specs(tm, D_MODEL) + [
                  pl.BlockSpec((1, D_MODEL), cst),
                  pl.BlockSpec((D_MODEL, 128), cst), pl.BlockSpec((1, 128), cst)],
        out_specs=[pl.BlockSpec((tm, D_MODEL), row), pl.BlockSpec((tm, D_MODEL), row),
                   pl.BlockSpec((tm, 128), row)],
        out_shape=[jax.ShapeDtypeStruct((NT, D_MODEL), F32), jax.ShapeDtypeStruct((NT, D_MODEL), F32),
                   jax.ShapeDtypeStruct((NT, 128), F32)],
        compiler_params=_cparams(("arbitrary",)),
        name="moe_router",
    )(x_p, x_s, attn_p, attn_s, g_ffn.reshape(1, D_MODEL), w_route, b_route)


def _row_gather_kernel(idx_ref, cnt_ref, src_ref, o_ref, buf, sem, *, rows):
    step = pl.program_id(0)
    base = step * rows
    n_real = cnt_ref[step]

    @pl.when(n_real < rows)
    def _():
        buf[...] = jnp.zeros_like(buf)

    def copy(r):
        return pltpu.make_async_copy(src_ref.at[pl.ds(idx_ref[base + r], 1), :], buf.at[pl.ds(r, 1), :], sem)

    def start_pair(k, c):
        copy(2 * k).start(priority=0)
        copy(2 * k + 1).start(priority=1)
        return c

    def wait(r, c):
        copy(r).wait()
        return c

    lax.fori_loop(0, n_real // 2, start_pair, 0)

    @pl.when(n_real % 2 == 1)
    def _():
        copy(n_real - 1).start(priority=0)

    lax.fori_loop(0, n_real, wait, 0)
    o_ref[...] = buf[...].astype(o_ref.dtype)


def _row_gather(src, idx, n_real, rows, dtype):
    n = idx.shape[0]
    d = src.shape[1]
    return pl.pallas_call(
        functools.partial(_row_gather_kernel, rows=rows),
        grid_spec=pltpu.PrefetchScalarGridSpec(
            num_scalar_prefetch=2,
            grid=(n // rows,),
            in_specs=[pl.BlockSpec(memory_space=pl.ANY)],
            out_specs=pl.BlockSpec((rows, d), lambda i, ix, cn: (i, 0)),
            scratch_shapes=[pltpu.VMEM((rows, d), src.dtype), pltpu.SemaphoreType.DMA(())],
        ),
        out_shape=jax.ShapeDtypeStruct((n, d), dtype),
        compiler_params=_cparams(("arbitrary",)),
        name="row_gather",
    )(idx, n_real, src)


GATHER_ROWS = 256
EXPERT_TM = 512
EXPERT_TF = 256
EXPERT_VMEM_LIMIT = 62 * 1024 * 1024
EXPERT_TN = 1024
N_ASSIGN = 2 * NT
EXPERT_TILES = N_ASSIGN // EXPERT_TM + N_EXPERTS
R_PAD = EXPERT_TILES * EXPERT_TM


def _expert_kernel(te_ref, tv_ref, xs_ref, wg_ref, wu_ref, wd_ref, rw_ref, o_ref):
    i = pl.program_id(0)
    j = pl.program_id(1)
    live = tv_ref[i] > 0

    @pl.when(live)
    def _():
        x = xs_ref[...]
        gate = _dot(x, wg_ref[...].astype(BF16))
        up = _dot(x, wu_ref[...].astype(BF16))
        hid = _silu(gate) * up * rw_ref[...]
        hb = hid.astype(BF16)

        @pl.when(j == 0)
        def _():
            o_ref[...] = jnp.zeros_like(o_ref)

        for c in range(D_MODEL // EXPERT_TN):
            cs = slice(c * EXPERT_TN, (c + 1) * EXPERT_TN)
            o_ref[:, cs] += _dot(hb, wd_ref[:, cs].astype(BF16))

    @pl.when(jnp.logical_not(live) & (j == 0))
    def _():
        o_ref[...] = jnp.zeros_like(o_ref)


def _experts(xs, roww, tile_expert, tile_live, w_g, w_u, w_d):
    tm, tf = EXPERT_TM, EXPERT_TF
    return pl.pallas_call(
        _expert_kernel,
        grid_spec=pltpu.PrefetchScalarGridSpec(
            num_scalar_prefetch=2,
            grid=(EXPERT_TILES, D_EXPERT // tf),
            in_specs=[pl.BlockSpec((tm, D_MODEL), lambda i, j, te, tv: (i, 0)),
                      pl.BlockSpec((None, D_MODEL, tf), lambda i, j, te, tv: (te[i], 0, j)),
                      pl.BlockSpec((None, D_MODEL, tf), lambda i, j, te, tv: (te[i], 0, j)),
                      pl.BlockSpec((None, tf, D_MODEL), lambda i, j, te, tv: (te[i], j, 0)),
                      pl.BlockSpec((tm, 1), lambda i, j, te, tv: (i, 0))],
            out_specs=pl.BlockSpec((tm, D_MODEL), lambda i, j, te, tv: (i, 0)),
        ),
        out_shape=jax.ShapeDtypeStruct((R_PAD, D_MODEL), F32),
        compiler_params=_cparams(("arbitrary", "arbitrary"), EXPERT_VMEM_LIMIT),
        name="moe_experts",
    )(tile_expert, tile_live, xs, w_g, w_u, w_d, roww)


def _combine_kernel(p0_ref, p1_ref, ys_ref, h_ref, g_ref, h2_ref, n_ref, buf0, buf1, sem, *, rows):
    base = pl.program_id(0) * rows

    def copies(r):
        return (pltpu.make_async_copy(ys_ref.at[pl.ds(p0_ref[base + r], 1), :], buf0.at[pl.ds(r, 1), :], sem.at[0]),
                pltpu.make_async_copy(ys_ref.at[pl.ds(p1_ref[base + r], 1), :], buf1.at[pl.ds(r, 1), :], sem.at[1]))

    def start(r, c):
        for prio, cp in enumerate(copies(r)):
            cp.start(priority=prio)
        return c

    def wait(r, c):
        for cp in copies(r):
            cp.wait()
        return c

    lax.fori_loop(0, rows, start, 0)
    lax.fori_loop(0, rows, wait, 0)
    h2 = h_ref[...] + (buf0[...] + buf1[...])
    h2_ref[...] = h2
    n_ref[...] = _rms_rows(h2, g_ref[...]).astype(BF16)


def _combine(ys, pos0, pos1, h, g_ple_in, rows):
    row = lambda i, a, b: (i, 0)
    return pl.pallas_call(
        functools.partial(_combine_kernel, rows=rows),
        grid_spec=pltpu.PrefetchScalarGridSpec(
            num_scalar_prefetch=2,
            grid=(NT // rows,),
            in_specs=[pl.BlockSpec(memory_space=pl.ANY), pl.BlockSpec((rows, D_MODEL), row),
                      pl.BlockSpec((1, D_MODEL), lambda i, a, b: (0, 0))],
            out_specs=[pl.BlockSpec((rows, D_MODEL), row), pl.BlockSpec((rows, D_MODEL), row)],
            scratch_shapes=[pltpu.VMEM((rows, D_MODEL), F32), pltpu.VMEM((rows, D_MODEL), F32),
                            pltpu.SemaphoreType.DMA((2,))],
        ),
        out_shape=[jax.ShapeDtypeStruct((NT, D_MODEL), F32), jax.ShapeDtypeStruct((NT, D_MODEL), BF16)],
        compiler_params=_cparams(("arbitrary",)),
        name="moe_combine",
    )(pos0, pos1, ys, h, g_ple_in.reshape(1, D_MODEL))


def _dispatch_plan(route):
    tm = EXPERT_TM
    a_exp = route[:, 0:2].astype(jnp.int32).reshape(-1)
    a_w = route[:, 2:4].reshape(-1)
    a_tok = jnp.arange(N_ASSIGN, dtype=jnp.int32) // 2
    order = jnp.argsort(a_exp, stable=True)
    counts = jnp.bincount(a_exp, length=N_EXPERTS).astype(jnp.int32)
    padded = -(-counts // tm) * tm
    pad_end = jnp.cumsum(padded)
    pad_start = pad_end - padded
    start = jnp.cumsum(counts) - counts
    sorted_rank = jnp.argsort(order).astype(jnp.int32)
    pos = pad_start[a_exp] + sorted_rank - start[a_exp]
    tile_row = jnp.arange(EXPERT_TILES, dtype=jnp.int32) * tm
    tile_live = (tile_row < pad_end[-1]).astype(jnp.int32)
    tile_expert = jnp.minimum(jnp.searchsorted(pad_end, tile_row, side="right"), N_EXPERTS - 1).astype(jnp.int32)
    slot = jnp.arange(R_PAD, dtype=jnp.int32)
    e_s = tile_expert[slot // tm]
    off = slot - pad_start[e_s]
    real = (off < counts[e_s]) & (slot < pad_end[-1])
    a_s = order[jnp.clip(start[e_s] + off, 0, N_ASSIGN - 1)]
    src = jnp.where(real, a_tok[a_s], 0).astype(jnp.int32)
    roww = jnp.where(real, a_w[a_s], 0.0)
    chunk_real = jnp.sum(real.reshape(R_PAD // GATHER_ROWS, GATHER_ROWS), axis=1).astype(jnp.int32)
    last_live = jnp.maximum(pad_end[-1] // tm - 1, 0)
    tile_expert = jnp.where(tile_live > 0, tile_expert, tile_expert[last_live])
    pos2 = pos.reshape(NT, 2)
    return src, chunk_real, roww.reshape(R_PAD, 1), tile_expert, tile_live, pos2[:, 0], pos2[:, 1]


def _sample_rows(sample):
    return jnp.pad(sample, ((0, 0), (0, SROW - DEC_SEQ), (0, 0))).reshape(NS, sample.shape[-1])


def _rows_of(prompt, sample):
    return jnp.concatenate([prompt.reshape(NP, prompt.shape[-1]), _sample_rows(sample)], axis=0)


def _kv_rows(z_p, z_s, col):
    prompt = z_p[:, col:col + KV_ROW].reshape(1, BATCH, SEQ, NSA_KV_HEADS, 2, HEAD_DIM)
    sample = z_s[:, col:col + KV_ROW].reshape(DEC_BATCH, SROW, NSA_KV_HEADS, 2, HEAD_DIM)[None, :, :DEC_SEQ]
    return prompt, sample


def kernel(x_prompt, x_sample, cache_cmp, cache_sel, cache_win, state_hgrn, page_table, p_prompt, p_sample, g_attn, w_in, lb_logits, g_lin_out, g_q, g_kc, g_ks, g_kw, cmp_pos_logits, cmp_w1, cmp_w2, g_nsa_out, w_out, g_ffn, w_group_router, b_group_router, w_expert_router, b_expert_router, w_exp_gate, w_exp_up, w_exp_down, g_ple_in, w_ple_gate, w_ple_proj, g_ple_out):
    l = 0
    x_p = x_prompt.reshape(NP, D_MODEL)
    x_s = _sample_rows(x_sample)
    p_rows = _rows_of(p_prompt[l], p_sample[l])

    a_p = _rms_cast(x_p, g_attn[l], 512, BF16)
    a_s = _rms_cast(x_s, g_attn[l], NS, F32)
    tn = 512
    ones = jnp.ones((HEAD_DIM,), F32)
    kv_gain = lambda gk: jnp.tile(jnp.concatenate([gk, ones]), NSA_KV_HEADS)
    kv_sel = jnp.tile(jnp.concatenate([ones, 0 * ones]), NSA_KV_HEADS)
    gain = jnp.concatenate([jnp.ones((COL_NQ,), F32), jnp.tile(g_q[l], NSA_HEADS), jnp.ones((KV_ROW,), F32),
                            kv_gain(g_ks[l]), kv_gain(g_kw[l])]).reshape(1, N_MAIN)
    normsel = jnp.concatenate([jnp.zeros((COL_NQ,), F32), jnp.ones((NSA_WIDTH,), F32), jnp.zeros((KV_ROW,), F32),
                               kv_sel, kv_sel]).reshape(1, N_MAIN)
    tile_norm = np.zeros((N_MAIN // tn,), np.int32)
    tile_norm[COL_NQ // tn:COL_KVC // tn] = 1
    tile_norm[COL_KVS // tn:] = 1
    flags = jnp.asarray(tile_norm)
    z_p = _matmul(a_p, w_in[l], N_MAIN, 1024, tn, "groupnorm", (gain, normsel), ("row", "row"), flags=flags,
                  name="in_proj")
    z_s = _matmul(a_s, w_in[l], N_MAIN, NS, tn, "groupnorm", (gain, normsel), ("row", "row"), flags=flags,
                  name="in_proj")
    w_gate = jnp.pad(w_in[l][:, COL_GATE:COL_GATE + N_GATE], ((0, 0), (0, 128 - N_GATE)))
    gates_p = _matmul(a_p, w_gate, 128, 1024, 128, "sigmoid", name="gate_proj")
    gates_s = _matmul(a_s, w_gate, 128, NS, 128, "sigmoid", name="gate_proj")

    lb = jnp.cumsum(jax.nn.softmax(lb_logits.astype(F32), axis=0), axis=0)[l]
    ylin_p, st_p = _hgrn(z_p, lb, g_lin_out[l], None, batch=BATCH, rows=SEQ, c=128, t_valid=128, hp=4,
                         precise=False)
    ylin_s, st_s = _hgrn(z_s, lb, g_lin_out[l], state_hgrn[l], batch=DEC_BATCH, rows=SROW, c=SROW,
                         t_valid=DEC_SEQ, hp=LIN_HEADS, precise=True)

    cmp_p, cmp_s = _kv_rows(z_p, z_s, COL_KVC)
    sel_p, sel_s = _kv_rows(z_p, z_s, COL_KVS)
    win_p, win_new = _kv_rows(z_p, z_s, COL_KVW)
    logits_x = jnp.broadcast_to(cmp_pos_logits[l].transpose(2, 0, 1).reshape(CMP_LEN, KV_SUB, 1),
                                (CMP_LEN, KV_SUB, HEAD_DIM))
    pt_flat = page_table.reshape(-1)
    ckv_p = _compress_prompt(cmp_p.reshape(BATCH, SEQ, KV_SUB, HEAD_DIM), logits_x, cmp_w1[l], cmp_w2[l], g_kc[l])
    new_cmp = z_s[:, COL_KVC:COL_KVC + KV_ROW].reshape(DEC_BATCH, SROW, KV_SUB, HEAD_DIM)
    ckv_s = _compress_sample(cache_cmp[l].reshape(-1, PAGE_SIZE, KV_SUB, HEAD_DIM), pt_flat, new_cmp, logits_x,
                             cmp_w1[l], cmp_w2[l], g_kc[l])
    ocmp_p, mask_p = _cmp_select(z_p, ckv_p.reshape(BATCH, -1, KV_ROW), batch=BATCH, rows=SEQ, tq=256,
                                 nb=SEQ // SEL_BLOCK, pos_base=0, want_idx=False, precise=False)
    nb_s = -(-(PAST_LEN + DEC_SEQ) // SEL_BLOCK)
    ocmp_s, _, idx_s = _cmp_select(z_s, ckv_s.reshape(DEC_BATCH, -1, KV_ROW), batch=DEC_BATCH, rows=SROW,
                                   tq=SROW, nb=nb_s, pos_base=PAST_LEN, want_idx=True, precise=True)
    osel_p = _flash_prompt(z_p, "sel", mask_p)
    owin_p = _flash_prompt(z_p, "win")
    idx = idx_s.reshape(DEC_BATCH, SROW, NSA_KV_HEADS, HEAD_DIM)[:, :DEC_SEQ, :, :N_SEL].astype(jnp.int32)
    blk_no = jnp.concatenate([idx[..., :2], idx[..., 3:]], axis=-1)
    page = jnp.take_along_axis(page_table[:, None, None, :], blk_no // 2, axis=-1)
    phys_half = page * 2 + blk_no % 2
    osel_s = _sel_sample(z_s, cache_sel[l], phys_half.reshape(-1), blk_no.reshape(-1))
    owin_s = _win_sample(z_s, cache_win[l].reshape(DEC_BATCH, -1, HEAD_DIM))

    y_p = _mix(ylin_p, ocmp_p, osel_p, owin_p, gates_p, g_nsa_out[l], 256, BF16)
    y_s = _mix(ylin_s, ocmp_s, osel_s, owin_s, gates_s, g_nsa_out[l], 256, F32)
    attn_p = _matmul(y_p, w_out[l], D_MODEL, 1024, 512, "none", name="out_proj")
    attn_s = _matmul(y_s, w_out[l], D_MODEL, NS, 512, "none", name="out_proj")

    w_route = jnp.pad(jnp.concatenate([w_group_router[l], w_expert_router[l]], axis=1),
                      ((0, 0), (0, 128 - N_GROUPS - N_EXPERTS)))
    b_route = jnp.pad(jnp.concatenate([b_group_router[l], b_expert_router[l]]),
                      (0, 128 - N_GROUPS - N_EXPERTS)).reshape(1, 128)
    h1, m, route = _router(x_p, x_s, attn_p, attn_s, g_ffn[l], w_route, b_route, 128)
    src, chunk_real, roww, tile_expert, tile_live, pos0, pos1 = _dispatch_plan(route)
    xs = _row_gather(m, src, chunk_real, GATHER_ROWS, BF16)
    ys = _experts(xs, roww, tile_expert, tile_live, w_exp_gate[l], w_exp_up[l], w_exp_down[l])
    h2, n2 = _combine(ys, pos0, pos1, h1, g_ple_in[l], 128)

    pp = _matmul(p_rows.astype(BF16), w_ple_proj[l], D_MODEL, 256, D_MODEL, "rowrms",
                 (g_ple_out[l].reshape(1, D_MODEL),), ("row",), name="ple_proj")
    out_p = _matmul(n2, w_ple_gate[l], D_MODEL, 1024, 512, "ple", (h2, pp), ("full", "full"), name="ple_gate",
                    row0=0, m=NP)
    out_s = _matmul(n2, w_ple_gate[l], D_MODEL, NS, 512, "ple", (h2, pp), ("full", "full"), name="ple_gate",
                    row0=NP, m=NS)

    y_prompt = out_p.reshape(BATCH, SEQ, D_MODEL)
    y_sample = out_s.reshape(DEC_BATCH, SROW, D_MODEL)[:, :DEC_SEQ]
    w_keep = cache_win.shape[2]
    win_prompt = win_p[:, :, SEQ - min(WINDOW, SEQ):]
    win_sample = jnp.concatenate([cache_win[l], win_new[0]], axis=1)[None, :, DEC_SEQ:DEC_SEQ + w_keep]
    return (y_prompt, y_sample, cmp_p, cmp_s, sel_p, sel_s, win_prompt, win_sample,
            st_p[None], st_s[None])
```
